```python
import jax
import jax.numpy as jnp
from jax import lax
import numpy as np


D_MODEL = 4096
BATCH = 1
SEQ = 8192
DEPTH = 1

CHUNK = 64
Q_BLOCK = 128
EPS = 1e-6
D_MIX = D_MODEL
LRU_WIDTH = D_MIX // 2
LRU_BLOCKS = 16
LRU_BLOCK_DIM = LRU_WIDTH // LRU_BLOCKS
CONV_WIDTH = 4
LRU_C = 8.0
MLA_HEADS = 16
QK_NOPE_DIM = 128
QK_ROPE_DIM = 64
QK_HEAD_DIM = QK_NOPE_DIM + QK_ROPE_DIM
V_HEAD_DIM = (D_MIX - LRU_WIDTH) // MLA_HEADS
Q_LORA_RANK = 1024
KV_LORA_RANK = 512
ROPE_THETA = 10000.0
D_FF = ((8 * D_MODEL // 3 + 255) // 256) * 256
IN_SPLITS = (LRU_WIDTH, LRU_WIDTH, Q_LORA_RANK, KV_LORA_RANK, QK_ROPE_DIM)
IN_COLS = sum(IN_SPLITS)
N_MOD = 6

kernel_name = 'hybrid_rglru_mla_sandwich_adaln_block'


def rms_norm(x, g):
    xf = x.astype(jnp.float32)
    y = xf * lax.rsqrt(jnp.mean(xf * xf, axis=-1, keepdims=True) + EPS)
    return (y * g.astype(jnp.float32)).astype(x.dtype)


def split_cols(t, sizes):
    offs = np.cumsum(sizes)[:-1].tolist()
    return jnp.split(t, offs, axis=-1)


def causal_depthwise_conv(x, w, b):
    S = x.shape[1]
    xp = jnp.pad(x, ((0, 0), (CONV_WIDTH - 1, 0), (0, 0)))
    y = b
    for k in range(CONV_WIDTH):
        y = y + xp[:, k:k + S, :] * w[k]
    return y


def block_diag_linear(x, w, b):
    B, S, _ = x.shape
    xb = x.reshape(B, S, LRU_BLOCKS, LRU_BLOCK_DIM)
    return jnp.einsum('bsni,nij->bsnj', xb, w).reshape(B, S, LRU_WIDTH) + b


def rg_lru(x, w_a, b_a, w_x, b_x, lam):
    r = jax.nn.sigmoid(block_diag_linear(x, w_a, b_a).astype(jnp.float32))
    i = jax.nn.sigmoid(block_diag_linear(x, w_x, b_x).astype(jnp.float32))
    log_a = -LRU_C * r * jax.nn.softplus(-lam.astype(jnp.float32))
    a = jnp.exp(log_a)
    u = jnp.sqrt(-jnp.expm1(2.0 * log_a)) * (i * x.astype(jnp.float32))

    def combine(left, right):
        a1, b1 = left
        a2, b2 = right
        return a1 * a2, a2 * b1 + b2

    _, h = lax.associative_scan(combine, (a, u), axis=1)
    return h.astype(x.dtype)


def rope_angles(positions):
    inv_freq = ROPE_THETA ** (-jnp.arange(0, QK_ROPE_DIM, 2, dtype=jnp.float32) / QK_ROPE_DIM)
    ang = positions.astype(jnp.float32)[..., None] * inv_freq
    return jnp.cos(ang), jnp.sin(ang)


def apply_rope(x, cos, sin):
    xf = x.astype(jnp.float32)
    x1, x2 = jnp.split(xf, 2, axis=-1)
    return jnp.concatenate([x1 * cos - x2 * sin, x2 * cos + x1 * sin], axis=-1).astype(x.dtype)


def mla(q_lat, kv_lat, k_rope, positions, g_q, w_q_up, g_kv, w_kv_up):
    B, S, _ = q_lat.shape
    q = jnp.dot(rms_norm(q_lat, g_q), w_q_up).reshape(B, S, MLA_HEADS, QK_HEAD_DIM)
    q_nope, q_pe = q[..., :QK_NOPE_DIM], q[..., QK_NOPE_DIM:]
    kv = jnp.dot(rms_norm(kv_lat, g_kv), w_kv_up).reshape(B, S, MLA_HEADS, QK_NOPE_DIM + V_HEAD_DIM)
    k_nope, v = kv[..., :QK_NOPE_DIM], kv[..., QK_NOPE_DIM:]
    cos, sin = rope_angles(positions)
    q_pe = apply_rope(q_pe, cos[:, :, None, :], sin[:, :, None, :])
    k_pe = apply_rope(k_rope, cos, sin)
    scale = QK_HEAD_DIM ** -0.5
    nqb = S // Q_BLOCK
    qn = q_nope.reshape(B, nqb, Q_BLOCK, MLA_HEADS, QK_NOPE_DIM).transpose(1, 0, 2, 3, 4)
    qp = q_pe.reshape(B, nqb, Q_BLOCK, MLA_HEADS, QK_ROPE_DIM).transpose(1, 0, 2, 3, 4)
    k_chunk = jnp.arange(S) // CHUNK

    def block(args):
        qb, qn_b, qp_b = args
        s = (jnp.einsum('bqhd,bkhd->bhqk', qn_b, k_nope)
             + jnp.einsum('bqhr,bkr->bhqk', qp_b, k_pe)).astype(jnp.float32) * scale
        q_chunk = (qb * Q_BLOCK + jnp.arange(Q_BLOCK)) // CHUNK
        mask = k_chunk[None, :] <= q_chunk[:, None]
        s = jnp.where(mask[None, None], s, -jnp.inf)
        p = jax.nn.softmax(s, axis=-1)
        return jnp.einsum('bhqk,bkhd->bqhd', p.astype(v.dtype), v)

    o = lax.map(block, (jnp.arange(nqb), qn, qp))
    return o.transpose(1, 0, 2, 3, 4).reshape(B, S, MLA_HEADS * V_HEAD_DIM)


def setup_inputs(seed: int = 0) -> dict:
    key = jax.random.key(seed)
    ks = jax.random.split(key, 32)

    def nrm(k, shape, scale):
        return jax.random.normal(k, shape, jnp.float32) * scale

    def gain(k, n):
        return 1.0 + 0.02 * jax.random.normal(k, (DEPTH, n), jnp.float32)

    x = jax.random.normal(ks[0], (BATCH, SEQ, D_MODEL), jnp.float32)
    c = jax.random.normal(ks[1], (BATCH, D_MODEL), jnp.float32)
    offset = jax.random.randint(ks[2], (BATCH, 1), 0, 4096, dtype=jnp.int32)
    positions = (offset + jnp.arange(SEQ, dtype=jnp.int32)[None, :]).astype(jnp.int32)
    u = jax.random.uniform(ks[3], (DEPTH, LRU_WIDTH), jnp.float32, minval=0.9, maxval=0.999)
    s = u ** (1.0 / LRU_C)
    lru_lambda = jnp.log(s) - jnp.log1p(-s)
    return {
        'x': x,
        'c': c,
        'positions': positions,
        'w_mod': nrm(ks[4], (DEPTH, D_MODEL, N_MOD * D_MODEL), 0.5 * D_MODEL ** -0.5),
        'b_mod': nrm(ks[5], (DEPTH, N_MOD * D_MODEL), 0.02),
        'g_pre_mix': gain(ks[6], D_MODEL),
        'w_in': nrm(ks[7], (DEPTH, D_MODEL, IN_COLS), D_MODEL ** -0.5),
        'conv_w': nrm(ks[8], (DEPTH, CONV_WIDTH, LRU_WIDTH), CONV_WIDTH ** -0.5),
        'conv_b': nrm(ks[9], (DEPTH, LRU_WIDTH), 0.02),
        'lru_w_a': nrm(ks[10], (DEPTH, LRU_BLOCKS, LRU_BLOCK_DIM, LRU_BLOCK_DIM), LRU_BLOCK_DIM ** -0.5),
        'lru_b_a': nrm(ks[11], (DEPTH, LRU_WIDTH), 0.02),
        'lru_w_x': nrm(ks[12], (DEPTH, LRU_BLOCKS, LRU_BLOCK_DIM, LRU_BLOCK_DIM), LRU_BLOCK_DIM ** -0.5),
        'lru_b_x': nrm(ks[13], (DEPTH, LRU_WIDTH), 0.02),
        'lru_lambda': lru_lambda,
        'g_q_lat': gain(ks[14], Q_LORA_RANK),
        'w_q_up': nrm(ks[15], (DEPTH, Q_LORA_RANK, MLA_HEADS * QK_HEAD_DIM), Q_LORA_RANK ** -0.5),
        'g_kv_lat': gain(ks[16], KV_LORA_RANK),
        'w_kv_up': nrm(ks[17], (DEPTH, KV_LORA_RANK, MLA_HEADS * (QK_NOPE_DIM + V_HEAD_DIM)), KV_LORA_RANK ** -0.5),
        'g_lru_out': gain(ks[18], LRU_WIDTH),
        'g_mla_out': gain(ks[19], MLA_HEADS * V_HEAD_DIM),
        'w_out': nrm(ks[20], (DEPTH, D_MIX, D_MODEL), D_MIX ** -0.5),
        'g_post_mix': gain(ks[21], D_MODEL),
        'g_pre_ffn': gain(ks[22], D_MODEL),
        'w_gate': nrm(ks[23], (DEPTH, D_MODEL, D_FF), D_MODEL ** -0.5),
        'w_up': nrm(ks[24], (DEPTH, D_MODEL, D_FF), D_MODEL ** -0.5),
        'w_down': nrm(ks[25], (DEPTH, D_FF, D_MODEL), D_FF ** -0.5),
        'g_post_ffn': gain(ks[26], D_MODEL),
    }


def reference(x, c, positions, w_mod, b_mod, g_pre_mix, w_in, conv_w, conv_b, lru_w_a, lru_b_a,
              lru_w_x, lru_b_x, lru_lambda, g_q_lat, w_q_up, g_kv_lat, w_kv_up, g_lru_out, g_mla_out,
              w_out, g_post_mix, g_pre_ffn, w_gate, w_up, w_down, g_post_ffn):
    for l in range(DEPTH):
        mod = jnp.dot(jax.nn.silu(c), w_mod[l]) + b_mod[l]
        sh_a, sc_a, gt_a, sh_f, sc_f, gt_f = [m[:, None, :] for m in jnp.split(mod, N_MOD, axis=-1)]

        h = rms_norm(x, g_pre_mix[l]) * (1.0 + sc_a) + sh_a
        proj = jnp.dot(h, w_in[l])
        xr, gr, q_lat, kv_lat, k_rope = split_cols(proj, IN_SPLITS)
        xr = causal_depthwise_conv(xr, conv_w[l], conv_b[l])
        y_lru = rg_lru(xr, lru_w_a[l], lru_b_a[l], lru_w_x[l], lru_b_x[l], lru_lambda[l]) * jax.nn.gelu(gr)
        y_mla = mla(q_lat, kv_lat, k_rope, positions, g_q_lat[l], w_q_up[l], g_kv_lat[l], w_kv_up[l])
        y = jnp.concatenate([rms_norm(y_lru, g_lru_out[l]), rms_norm(y_mla, g_mla_out[l])], axis=-1)
        y = jnp.dot(y, w_out[l])
        x = x + gt_a * rms_norm(y, g_post_mix[l])

        h = rms_norm(x, g_pre_ffn[l]) * (1.0 + sc_f) + sh_f
        f = jnp.dot(jax.nn.silu(jnp.dot(h, w_gate[l])) * jnp.dot(h, w_up[l]), w_down[l])
        x = x + gt_f * rms_norm(f, g_post_ffn[l])
    return x
```

```python
import functools

import jax
import jax.numpy as jnp
from jax import lax
from jax.experimental import pallas as pl
from jax.experimental.pallas import tpu as pltpu

F32 = jnp.float32
BF16 = jnp.bfloat16

EPS = 1e-6
CHUNK = 64
CONV_WIDTH = 4
LRU_C = 8.0
QK_NOPE_DIM = 128
QK_ROPE_DIM = 64
V_HEAD_DIM = 128
ROPE_THETA = 10000.0
N_MOD = 6

LANES = 128
SUBLANES = 8
MXU_DIM = 256
VMEM_LIMIT_BYTES = 56 * 1024 * 1024

QK_PAD_DIM = MXU_DIM
ROW_CHUNK = 128


def _cparams(*sem):
    return pltpu.CompilerParams(dimension_semantics=sem, vmem_limit_bytes=VMEM_LIMIT_BYTES)


def _blk(pref, dim):
    b = min(pref, dim)
    while dim % b:
        b //= 2
    return b


def _rms(x, g):
    ms = jnp.mean(x * x, axis=-1, keepdims=True)
    return x * lax.rsqrt(ms + EPS) * g


def _silu(x):
    return x * jax.nn.sigmoid(x)


def _mod_kernel(c_ref, w_ref, b_ref, o_ref):
    k = pl.program_id(1)
    s = _silu(c_ref[...])
    part = jnp.sum(w_ref[...] * s, axis=0, keepdims=True)

    @pl.when(k == 0)
    def _():
        o_ref[...] = b_ref[...] + part

    @pl.when(k != 0)
    def _():
        o_ref[...] += part


def _mod_call(c_col, w_mod, b_mod):
    d, n = w_mod.shape
    tk, bn = _blk(1024, d), _blk(2048, n)
    return pl.pallas_call(
        _mod_kernel,
        grid=(n // bn, d // tk),
        in_specs=[pl.BlockSpec((tk, 1), lambda j, k: (k, 0)),
                  pl.BlockSpec((tk, bn), lambda j, k: (k, j)),
                  pl.BlockSpec((1, bn), lambda j, k: (0, j))],
        out_specs=pl.BlockSpec((1, bn), lambda j, k: (0, j)),
        out_shape=jax.ShapeDtypeStruct((1, n), F32),
        compiler_params=_cparams("parallel", "arbitrary"),
        name="mod",
    )(c_col, w_mod, b_mod)


def _in_proj_kernel(x_ref, g_ref, sh_ref, sc_ref, w_ref, o_ref, h_ref):
    @pl.when(pl.program_id(1) == 0)
    def _():
        g = g_ref[...]
        sc = 1.0 + sc_ref[...]
        sh = sh_ref[...]
        for r in range(0, x_ref.shape[0], ROW_CHUNK):
            rows = pl.ds(r, ROW_CHUNK)
            h_ref[rows, :] = (_rms(x_ref[rows, :], g) * sc + sh).astype(BF16)

    o_ref[...] = jnp.dot(h_ref[...], w_ref[...], preferred_element_type=F32).astype(o_ref.dtype)


def _in_proj_call(x, g, mod, w):
    s, d = x.shape
    n = w.shape[1]
    bm, bn = _blk(512, s), _blk(1024, n)
    return pl.pallas_call(
        _in_proj_kernel,
        grid=(s // bm, n // bn),
        in_specs=[pl.BlockSpec((bm, d), lambda i, j: (i, 0)),
                  pl.BlockSpec((1, d), lambda i, j: (0, 0)),
                  pl.BlockSpec((1, d), lambda i, j: (0, 0)),
                  pl.BlockSpec((1, d), lambda i, j: (0, 1)),
                  pl.BlockSpec((d, bn), lambda i, j: (0, j))],
        out_specs=pl.BlockSpec((bm, bn), lambda i, j: (i, j)),
        out_shape=jax.ShapeDtypeStruct((s, n), BF16),
        scratch_shapes=[pltpu.VMEM((bm, d), BF16)],
        compiler_params=_cparams("parallel", "arbitrary"),
        name="in_proj",
    )(x, g, mod, mod, w)


def _rope_kernel(pos_ref, invf_ref, sign_ref, o_ref):
    ang = pos_ref[...].astype(F32) * invf_ref[...]
    lane = lax.broadcasted_iota(jnp.int32, ang.shape, 1)
    o_ref[...] = jnp.where(lane < QK_ROPE_DIM, jnp.cos(ang), jnp.sin(ang) * sign_ref[...])


def _rope_call(pos_col):
    s = pos_col.shape[0]
    half = QK_ROPE_DIM // 2
    inv_freq = ROPE_THETA ** (-jnp.arange(0, QK_ROPE_DIM, 2, dtype=F32) / QK_ROPE_DIM)
    invf = jnp.tile(inv_freq, 4)[None, :]
    sign = jnp.concatenate([jnp.ones((3 * half,), F32), jnp.ones((half,), F32)])
    sign = sign.at[2 * half:3 * half].set(-1.0)[None, :]
    ts = _blk(1024, s)
    return pl.pallas_call(
        _rope_kernel,
        grid=(s // ts,),
        in_specs=[pl.BlockSpec((ts, 1), lambda i: (i, 0)),
                  pl.BlockSpec((1, LANES), lambda i: (0, 0)),
                  pl.BlockSpec((1, LANES), lambda i: (0, 0))],
        out_specs=pl.BlockSpec((ts, LANES), lambda i: (i, 0)),
        out_shape=jax.ShapeDtypeStruct((s, LANES), F32),
        compiler_params=_cparams("parallel"),
        name="rope_tab",
    )(pos_col, invf, sign)


def _lru_kernel(xr_ref, gr_ref, cw_ref, cb_ref, wax_ref, ba_ref, bx_ref, lam_ref, o_ref,
                xbuf, a_scr, b_scr, carry_scr):
    t = pl.program_id(1)
    tlen = xr_ref.shape[0]
    ngrp = tlen // SUBLANES

    @pl.when(t == 0)
    def _():
        xbuf[0:SUBLANES, :] = jnp.zeros((SUBLANES, LANES), F32)
        carry_scr[...] = jnp.zeros((SUBLANES, LANES), F32)

    x = xr_ref[...].astype(F32)
    xbuf[SUBLANES:SUBLANES + tlen, :] = x
    cw = cw_ref[...]
    xc = cb_ref[...]
    for k in range(CONV_WIDTH - 1):
        off = SUBLANES - (CONV_WIDTH - 1) + k
        xc = xc + xbuf[pl.ds(off, tlen), :] * cw[k:k + 1, :]
    xc = xc + x * cw[CONV_WIDTH - 1:CONV_WIDTH, :]
    xbuf[0:SUBLANES, :] = xbuf[tlen:tlen + SUBLANES, :]

    ri = jnp.dot(xc.astype(BF16), wax_ref[0], preferred_element_type=F32)
    r = jax.nn.sigmoid(ri[:, :LANES] + ba_ref[...])
    ig = jax.nn.sigmoid(ri[:, LANES:] + bx_ref[...])
    lam = lam_ref[...]
    softplus_neg_lam = jnp.maximum(-lam, 0.0) + jnp.log1p(jnp.exp(-jnp.abs(lam)))
    log_a = (-LRU_C * r) * softplus_neg_lam
    a = jnp.exp(log_a)
    u = jnp.sqrt(-jnp.tanh(log_a) * (1.0 + a * a)) * (ig * xc)

    av = a.reshape(ngrp, SUBLANES, LANES)
    bv = u.reshape(ngrp, SUBLANES, LANES)
    row = lax.broadcasted_iota(jnp.int32, av.shape, 1)
    for sh in (1, 2, 4):
        valid = row >= sh
        bv = jnp.where(valid, av * pltpu.roll(bv, sh, axis=1) + bv, bv)
        av = jnp.where(valid, av * pltpu.roll(av, sh, axis=1), av)
    a_scr[...] = av
    b_scr[...] = bv

    def group(g, carry):
        h = a_scr[g] * carry + b_scr[g]
        b_scr[g] = h
        return jnp.broadcast_to(h[SUBLANES - 1:SUBLANES, :], (SUBLANES, LANES))

    carry_scr[...] = lax.fori_loop(0, ngrp, group, carry_scr[...], unroll=8)
    h = b_scr[...].reshape(tlen, LANES)

    gt = gr_ref[...].astype(F32)
    gelu = 0.5 * gt * (1.0 + jnp.tanh(0.7978845608028654 * (gt + 0.044715 * (gt * gt * gt))))
    o_ref[...] = (h * gelu).astype(o_ref.dtype)


def _lru_call(proj, conv_w, conv_b, wax, b_a, b_x, lam, width):
    s = proj.shape[0]
    nblk = width // LANES
    ts = _blk(512, s)
    vec = lambda: pl.BlockSpec((1, LANES), lambda c, t: (0, c))
    return pl.pallas_call(
        _lru_kernel,
        grid=(nblk, s // ts),
        in_specs=[pl.BlockSpec((ts, LANES), lambda c, t: (t, c)),
                  pl.BlockSpec((ts, LANES), lambda c, t: (t, nblk + c)),
                  pl.BlockSpec((CONV_WIDTH, LANES), lambda c, t: (0, c)),
                  vec(),
                  pl.BlockSpec((1, LANES, 2 * LANES), lambda c, t: (c, 0, 0)),
                  vec(), vec(), vec()],
        out_specs=pl.BlockSpec((ts, LANES), lambda c, t: (t, c)),
        out_shape=jax.ShapeDtypeStruct((s, width), BF16),
        scratch_shapes=[pltpu.VMEM((ts + SUBLANES, LANES), F32),
                        pltpu.VMEM((ts // SUBLANES, SUBLANES, LANES), F32),
                        pltpu.VMEM((ts // SUBLANES, SUBLANES, LANES), F32),
                        pltpu.VMEM((SUBLANES, LANES), F32)],
        compiler_params=_cparams("parallel", "arbitrary"),
        name="lru",
    )(proj, proj, conv_w, conv_b, wax, b_a, b_x, lam)


def _rope_pair(pair, cs):
    prod = pair * cs
    return prod + pltpu.roll(prod, QK_ROPE_DIM, axis=1)


def _qkv_kernel(ql_ref, kvl_ref, kr_ref, cs_ref, gq_ref, gkv_ref, wq_ref, wkv_ref,
                q_ref, k_ref, v_ref, qn_scr, kvn_scr, krot_scr, *, scale):
    @pl.when(pl.program_id(1) == 0)
    def _():
        gq, gkv = gq_ref[...], gkv_ref[...]
        for r in range(0, ql_ref.shape[0], ROW_CHUNK):
            rows = pl.ds(r, ROW_CHUNK)
            qn_scr[rows, :] = _rms(ql_ref[rows, :].astype(F32), gq).astype(BF16)
            kvn_scr[rows, :] = _rms(kvl_ref[rows, :].astype(F32), gkv).astype(BF16)
            rot = _rope_pair(kr_ref[rows, :].astype(F32), cs_ref[rows, :])
            lane = lax.broadcasted_iota(jnp.int32, rot.shape, 1)
            krot_scr[rows, :] = jnp.where(lane < QK_ROPE_DIM, rot, 0.0).astype(BF16)

    qh = jnp.dot(qn_scr[...], wq_ref[0], preferred_element_type=F32)
    q_ref[0, :, :QK_NOPE_DIM] = (qh[:, :QK_NOPE_DIM] * scale).astype(BF16)
    q_ref[0, :, QK_NOPE_DIM:] = (_rope_pair(qh[:, QK_NOPE_DIM:], cs_ref[...]) * scale).astype(BF16)
    kvh = jnp.dot(kvn_scr[...], wkv_ref[0], preferred_element_type=F32)
    k_ref[0, :, :QK_NOPE_DIM] = kvh[:, :QK_NOPE_DIM].astype(BF16)
    k_ref[0, :, QK_NOPE_DIM:] = krot_scr[...]
    v_ref[0] = kvh[:, QK_NOPE_DIM:].astype(BF16)


def _qkv_call(proj, cs, g_q, g_kv, wq, wkv, q_off, kv_off, kr_off):
    s = proj.shape[0]
    heads, rq, _ = wq.shape
    rkv = wkv.shape[1]
    bm = _blk(512, s)
    scale = float(QK_NOPE_DIM + QK_ROPE_DIM) ** -0.5
    return pl.pallas_call(
        functools.partial(_qkv_kernel, scale=scale),
        grid=(s // bm, heads),
        in_specs=[pl.BlockSpec((bm, rq), lambda i, h: (i, q_off // rq)),
                  pl.BlockSpec((bm, rkv), lambda i, h: (i, kv_off // rkv)),
                  pl.BlockSpec((bm, LANES), lambda i, h: (i, kr_off // LANES)),
                  pl.BlockSpec((bm, LANES), lambda i, h: (i, 0)),
                  pl.BlockSpec((1, rq), lambda i, h: (0, 0)),
                  pl.BlockSpec((1, rkv), lambda i, h: (0, 0)),
                  pl.BlockSpec((1, rq, QK_PAD_DIM), lambda i, h: (h, 0, 0)),
                  pl.BlockSpec((1, rkv, QK_NOPE_DIM + V_HEAD_DIM), lambda i, h: (h, 0, 0))],
        out_specs=[pl.BlockSpec((1, bm, QK_PAD_DIM), lambda i, h: (h, i, 0)),
                   pl.BlockSpec((1, bm, QK_PAD_DIM), lambda i, h: (h, i, 0)),
                   pl.BlockSpec((1, bm, V_HEAD_DIM), lambda i, h: (h, i, 0))],
        out_shape=[jax.ShapeDtypeStruct((heads, s, QK_PAD_DIM), BF16),
                   jax.ShapeDtypeStruct((heads, s, QK_PAD_DIM), BF16),
                   jax.ShapeDtypeStruct((heads, s, V_HEAD_DIM), BF16)],
        scratch_shapes=[pltpu.VMEM((bm, rq), BF16), pltpu.VMEM((bm, rkv), BF16),
                        pltpu.VMEM((bm, LANES), BF16)],
        compiler_params=_cparams("parallel", "arbitrary"),
        name="qkv_proj",
    )(proj, proj, proj, cs, g_q, g_kv, wq, wkv)


def _attn_kernel(q_ref, k_ref, v_ref, o_ref, *, blk):
    i = pl.program_id(1)
    q = q_ref[0]

    def step(j, carry, masked):
        m, l, acc = carry
        rows = pl.ds(pl.multiple_of(j * blk, blk), blk)
        s = lax.dot_general(q, k_ref[0, rows, :], (((1,), (1,)), ((), ())),
                            preferred_element_type=F32)
        if masked:
            qc = lax.broadcasted_iota(jnp.int32, s.shape, 0) // CHUNK
            kc = lax.broadcasted_iota(jnp.int32, s.shape, 1) // CHUNK
            s = jnp.where(kc <= qc, s, -jnp.inf)
        m_new = jnp.maximum(m, jnp.max(s, axis=-1, keepdims=True))
        alpha = jnp.exp(m - m_new)
        p = jnp.exp(s - m_new)
        l = alpha * l + jnp.sum(p, axis=-1, keepdims=True)
        acc = alpha * acc + jnp.dot(p.astype(BF16), v_ref[0, rows, :], preferred_element_type=F32)
        return m_new, l, acc

    init = (jnp.full((blk, 1), -jnp.inf, F32), jnp.zeros((blk, 1), F32),
            jnp.zeros((blk, V_HEAD_DIM), F32))
    carry = lax.fori_loop(0, i, lambda j, c: step(j, c, False), init)
    _, l, acc = step(i, carry, True)
    o_ref[...] = (acc / l).astype(o_ref.dtype)


def _attn_call(q, k, v):
    heads, s, _ = q.shape
    blk = _blk(512, s)
    assert blk % CHUNK == 0
    return pl.pallas_call(
        functools.partial(_attn_kernel, blk=blk),
        grid=(heads, s // blk),
        in_specs=[pl.BlockSpec((1, blk, QK_PAD_DIM), lambda h, i: (h, i, 0)),
                  pl.BlockSpec((1, s, QK_PAD_DIM), lambda h, i: (h, 0, 0)),
                  pl.BlockSpec((1, s, V_HEAD_DIM), lambda h, i: (h, 0, 0))],
        out_specs=pl.BlockSpec((blk, V_HEAD_DIM), lambda h, i: (i, h)),
        out_shape=jax.ShapeDtypeStruct((s, heads * V_HEAD_DIM), BF16),
        compiler_params=_cparams("parallel", "arbitrary"),
        name="attn",
    )(q, k, v)


def _out_proj_kernel(yl_ref, ym_ref, gl_ref, gm_ref, w_ref, o_ref, lhs_ref):
    width = yl_ref.shape[1]

    @pl.when(pl.program_id(1) == 0)
    def _():
        gl, gm = gl_ref[...], gm_ref[...]
        for r in range(0, yl_ref.shape[0], ROW_CHUNK):
            rows = pl.ds(r, ROW_CHUNK)
            lhs_ref[rows, :width] = _rms(yl_ref[rows, :].astype(F32), gl).astype(BF16)
            lhs_ref[rows, width:] = _rms(ym_ref[rows, :].astype(F32), gm).astype(BF16)

    o_ref[...] = jnp.dot(lhs_ref[...], w_ref[...], preferred_element_type=F32)


def _out_proj_call(y_lru, y_mla, g_lru, g_mla, w):
    s, width = y_lru.shape
    wm = y_mla.shape[1]
    k, n = w.shape
    bm, bn = _blk(512, s), _blk(1024, n)
    return pl.pallas_call(
        _out_proj_kernel,
        grid=(s // bm, n // bn),
        in_specs=[pl.BlockSpec((bm, width), lambda i, j: (i, 0)),
                  pl.BlockSpec((bm, wm), lambda i, j: (i, 0)),
                  pl.BlockSpec((1, width), lambda i, j: (0, 0)),
                  pl.BlockSpec((1, wm), lambda i, j: (0, 0)),
                  pl.BlockSpec((k, bn), lambda i, j: (0, j))],
        out_specs=pl.BlockSpec((bm, bn), lambda i, j: (i, j)),
        out_shape=jax.ShapeDtypeStruct((s, n), F32),
        scratch_shapes=[pltpu.VMEM((bm, k), BF16)],
        compiler_params=_cparams("parallel", "arbitrary"),
        name="out_proj",
    )(y_lru, y_mla, g_lru, g_mla, w)


def _post_mix_kernel(x_ref, y_ref, gpost_ref, gpre_ref, gt_ref, sh_ref, sc_ref, x1_ref, h_ref):
    x1 = x_ref[...] + gt_ref[...] * _rms(y_ref[...], gpost_ref[...])
    x1_ref[...] = x1
    h_ref[...] = (_rms(x1, gpre_ref[...]) * (1.0 + sc_ref[...]) + sh_ref[...]).astype(BF16)


def _post_mix_call(x, y, g_post, g_pre, mod):
    s, d = x.shape
    bm = _blk(256, s)
    row = pl.BlockSpec((bm, d), lambda i: (i, 0))
    chunk = lambda c: pl.BlockSpec((1, d), lambda i: (0, c))
    return pl.pallas_call(
        _post_mix_kernel,
        grid=(s // bm,),
        in_specs=[row, row, chunk(0), chunk(0), chunk(2), chunk(3), chunk(4)],
        out_specs=[row, row],
        out_shape=[jax.ShapeDtypeStruct((s, d), F32), jax.ShapeDtypeStruct((s, d), BF16)],
        compiler_params=_cparams("parallel"),
        name="post_mix",
    )(x, y, g_post, g_pre, mod, mod, mod)


def _gate_up_kernel(h_ref, wg_ref, wu_ref, o_ref):
    h = h_ref[...]
    g = jnp.dot(h, wg_ref[...], preferred_element_type=F32)
    u = jnp.dot(h, wu_ref[...], preferred_element_type=F32)
    o_ref[...] = (_silu(g) * u).astype(o_ref.dtype)


def _gate_up_call(h, wg, wu):
    s, d = h.shape
    f = wg.shape[1]
    bm, bn = _blk(1024, s), _blk(512, f)
    return pl.pallas_call(
        _gate_up_kernel,
        grid=(s // bm, f // bn),
        in_specs=[pl.BlockSpec((bm, d), lambda i, j: (i, 0)),
                  pl.BlockSpec((d, bn), lambda i, j: (0, j)),
                  pl.BlockSpec((d, bn), lambda i, j: (0, j))],
        out_specs=pl.BlockSpec((bm, bn), lambda i, j: (i, j)),
        out_shape=jax.ShapeDtypeStruct((s, f), BF16),
        compiler_params=_cparams("parallel", "arbitrary"),
        name="gate_up",
    )(h, wg, wu)


def _down_kernel(a_ref, w_ref, o_ref):
    part = jnp.dot(a_ref[...], w_ref[...], preferred_element_type=F32)

    @pl.when(pl.program_id(2) == 0)
    def _():
        o_ref[...] = part

    @pl.when(pl.program_id(2) != 0)
    def _():
        o_ref[...] += part


def _down_call(a, w, bk):
    s, f = a.shape
    n = w.shape[1]
    bm, bn = _blk(1024, s), _blk(1024, n)
    return pl.pallas_call(
        _down_kernel,
        grid=(s // bm, n // bn, f // bk),
        in_specs=[pl.BlockSpec((bm, bk), lambda i, j, k: (i, k)),
                  pl.BlockSpec((bk, bn), lambda i, j, k: (k, j))],
        out_specs=pl.BlockSpec((bm, bn), lambda i, j, k: (i, j)),
        out_shape=jax.ShapeDtypeStruct((s, n), F32),
        compiler_params=_cparams("parallel", "parallel", "arbitrary"),
        name="down",
    )(a, w)


def _final_kernel(x_ref, f_ref, g_ref, gt_ref, o_ref):
    o_ref[...] = x_ref[...] + gt_ref[...] * _rms(f_ref[...], g_ref[...])


def _final_call(x1, f, g, mod):
    s, d = x1.shape
    bm = _blk(256, s)
    row = pl.BlockSpec((bm, d), lambda i: (i, 0))
    return pl.pallas_call(
        _final_kernel,
        grid=(s // bm,),
        in_specs=[row, row, pl.BlockSpec((1, d), lambda i: (0, 0)),
                  pl.BlockSpec((1, d), lambda i: (0, 5))],
        out_specs=row,
        out_shape=jax.ShapeDtypeStruct((s, d), F32),
        compiler_params=_cparams("parallel"),
        name="final",
    )(x1, f, g, mod)


def _pad_cols(w, n):
    return jnp.pad(w, ((0, 0), (0, n - w.shape[1])))


def _round_up(v, m):
    return (v + m - 1) // m * m


def _swap_halves(w):
    half = w.shape[-1] // 2
    return jnp.concatenate([w[..., half:], w[..., :half]], axis=-1)


def _layer(x, c_col, cs, p):
    s, d = x.shape
    width = p["conv_w"].shape[-1]
    rq = p["g_q_lat"].shape[-1]
    rkv = p["g_kv_lat"].shape[-1]
    heads = p["w_kv_up"].shape[-1] // (QK_NOPE_DIM + V_HEAD_DIM)
    qk_dim = QK_NOPE_DIM + QK_ROPE_DIM
    d_ff = p["w_gate"].shape[-1]

    mod = _mod_call(c_col, p["w_mod"], p["b_mod"][None, :])

    w_in = p["w_in"]
    kr_off = 2 * width + rq + rkv
    w_in_ext = jnp.concatenate([w_in, _swap_halves(w_in[:, kr_off:])], axis=1)
    w_in_ext = _pad_cols(w_in_ext, _round_up(w_in_ext.shape[1], 4 * MXU_DIM)).astype(BF16)
    proj = _in_proj_call(x, p["g_pre_mix"][None, :], mod, w_in_ext)

    wax = jnp.concatenate([p["lru_w_a"], p["lru_w_x"]], axis=-1).astype(BF16)
    y_lru = _lru_call(proj, p["conv_w"], p["conv_b"][None, :], wax, p["lru_b_a"][None, :],
                      p["lru_b_x"][None, :], p["lru_lambda"][None, :], width)

    wq = p["w_q_up"].reshape(rq, heads, qk_dim).transpose(1, 0, 2)
    wq = jnp.concatenate([wq, _swap_halves(wq[..., QK_NOPE_DIM:])], axis=-1).astype(BF16)
    wkv = p["w_kv_up"].reshape(rkv, heads, QK_NOPE_DIM + V_HEAD_DIM).transpose(1, 0, 2).astype(BF16)
    q, k, v = _qkv_call(proj, cs, p["g_q_lat"][None, :], p["g_kv_lat"][None, :], wq, wkv,
                        2 * width, 2 * width + rq, kr_off)
    y_mla = _attn_call(q, k, v)

    y = _out_proj_call(y_lru, y_mla, p["g_lru_out"][None, :], p["g_mla_out"][None, :],
                       p["w_out"].astype(BF16))
    x1, h2 = _post_mix_call(x, y, p["g_post_mix"][None, :], p["g_pre_ffn"][None, :], mod)

    f_pad = _round_up(d_ff, 4 * MXU_DIM)
    wg = _pad_cols(p["w_gate"], f_pad).astype(BF16)
    wu = _pad_cols(p["w_up"], f_pad).astype(BF16)
    wd = jnp.pad(p["w_down"], ((0, f_pad - d_ff), (0, 0))).astype(BF16)
    act = _gate_up_call(h2, wg, wu)
    f = _down_call(act, wd, f_pad // 4)
    return _final_call(x1, f, p["g_post_ffn"][None, :], mod)


def kernel(x, c, positions, w_mod, b_mod, g_pre_mix, w_in, conv_w, conv_b, lru_w_a, lru_b_a, lru_w_x,
           lru_b_x, lru_lambda, g_q_lat, w_q_up, g_kv_lat, w_kv_up, g_lru_out, g_mla_out, w_out,
           g_post_mix, g_pre_ffn, w_gate, w_up, w_down, g_post_ffn):
    params = dict(w_mod=w_mod, b_mod=b_mod, g_pre_mix=g_pre_mix, w_in=w_in, conv_w=conv_w, conv_b=conv_b,
                  lru_w_a=lru_w_a, lru_b_a=lru_b_a, lru_w_x=lru_w_x, lru_b_x=lru_b_x, lru_lambda=lru_lambda,
                  g_q_lat=g_q_lat, w_q_up=w_q_up, g_kv_lat=g_kv_lat, w_kv_up=w_kv_up, g_lru_out=g_lru_out,
                  g_mla_out=g_mla_out, w_out=w_out, g_post_mix=g_post_mix, g_pre_ffn=g_pre_ffn,
                  w_gate=w_gate, w_up=w_up, w_down=w_down, g_post_ffn=g_post_ffn)
    batch, depth = x.shape[0], w_mod.shape[0]
    outs = []
    for b in range(batch):
        xb = x[b]
        c_col = c[b][:, None]
        cs = _rope_call(positions[b][:, None])
        for l in range(depth):
            xb = _layer(xb, c_col, cs, {name: w[l] for name, w in params.items()})
        outs.append(xb)
    return jnp.stack(outs, axis=0)
```

```python
import functools
import math

import jax
import jax.numpy as jnp
from jax import lax
from jax.experimental import pallas as pl
from jax.experimental.pallas import tpu as pltpu

F32 = jnp.float32
BF16 = jnp.bfloat16

EPS = 1e-6
CHUNK = 64
CONV_WIDTH = 4
LRU_C = 8.0
QK_NOPE_DIM = 128
QK_ROPE_DIM = 64
V_HEAD_DIM = 128
ROPE_THETA = 10000.0
N_MOD = 6

LANES = 128
SUBLANES = 8
BF16_ROWS = 16
MXU_DIM = 256
VMEM_LIMIT_BYTES = 56 * 1024 * 1024

QK_PAD_DIM = MXU_DIM
ROW_CHUNK = 128
ATTN_BLOCK = 512

NT_DIMS = (((1,), (1,)), ((), ()))


def _cparams(*sem):
    return pltpu.CompilerParams(dimension_semantics=sem, vmem_limit_bytes=VMEM_LIMIT_BYTES)


def _blk(pref, dim):
    b = min(pref, dim)
    while dim % b:
        b //= 2
    return b


def _rms(x, g):
    ms = jnp.mean(x * x, axis=-1, keepdims=True)
    return x * lax.rsqrt(ms + EPS) * g


def _sigmoid(x):
    return 0.5 * (1.0 + jnp.tanh(0.5 * x))


def _silu(x):
    return x * _sigmoid(x)


def _cast_rows_specs(w, n_steps, step_of):
    rows, cols = w.shape
    assert rows % n_steps == 0 and (rows // n_steps) % BF16_ROWS == 0, (w.shape, n_steps)
    slab = rows // n_steps
    spec = pl.BlockSpec((slab, cols), lambda *ids: (step_of(*ids), 0))
    return spec, spec, jax.ShapeDtypeStruct((rows, cols), BF16)


def _mod_kernel(c_ref, w_ref, b_ref, o_ref):
    k = pl.program_id(1)
    c = c_ref[...]
    s = c * jax.nn.sigmoid(c)
    part = jnp.sum(w_ref[...] * s, axis=0, keepdims=True)

    @pl.when(k == 0)
    def _():
        o_ref[...] = b_ref[...] + part

    @pl.when(k != 0)
    def _():
        o_ref[...] += part


def _mod_call(c_col, w_mod, b_mod):
    d, n = w_mod.shape
    tk, bn = _blk(1024, d), _blk(2048, n)
    return pl.pallas_call(
        _mod_kernel,
        grid=(n // bn, d // tk),
        in_specs=[pl.BlockSpec((tk, 1), lambda j, k: (k, 0)),
                  pl.BlockSpec((tk, bn), lambda j, k: (k, j)),
                  pl.BlockSpec((1, bn), lambda j, k: (0, j))],
        out_specs=pl.BlockSpec((1, bn), lambda j, k: (0, j)),
        out_shape=jax.ShapeDtypeStruct((1, n), F32),
        compiler_params=_cparams("parallel", "arbitrary"),
        name="mod",
    )(c_col, w_mod, b_mod)


def _in_proj_kernel(x_ref, g_ref, sh_ref, sc_ref, w_ref, wkr_ref, o_ref, kr_ref, h_ref):
    @pl.when(pl.program_id(1) == 0)
    def _():
        g = g_ref[...]
        sc = 1.0 + sc_ref[...]
        sh = sh_ref[...]
        for r in range(0, x_ref.shape[0], ROW_CHUNK):
            rows = pl.ds(r, ROW_CHUNK)
            h_ref[rows, :] = (_rms(x_ref[rows, :], g) * sc + sh).astype(BF16)
        kr_ref[...] = jnp.dot(h_ref[...], wkr_ref[...], preferred_element_type=F32)

    o_ref[...] = jnp.dot(h_ref[...], w_ref[...], preferred_element_type=F32).astype(o_ref.dtype)


def _in_proj_call(x, g, mod, w, wkr, n_main):
    s, d = x.shape
    bm, bn = _blk(512, s), _blk(512, n_main)
    return pl.pallas_call(
        _in_proj_kernel,
        grid=(s // bm, n_main // bn),
        in_specs=[pl.BlockSpec((bm, d), lambda i, j: (i, 0)),
                  pl.BlockSpec((1, d), lambda i, j: (0, 0)),
                  pl.BlockSpec((1, d), lambda i, j: (0, 0)),
                  pl.BlockSpec((1, d), lambda i, j: (0, 1)),
                  pl.BlockSpec((d, bn), lambda i, j: (0, j)),
                  pl.BlockSpec((d, LANES), lambda i, j: (0, 0))],
        out_specs=[pl.BlockSpec((bm, bn), lambda i, j: (i, j)),
                   pl.BlockSpec((bm, LANES), lambda i, j: (i, 0))],
        out_shape=[jax.ShapeDtypeStruct((s, n_main), BF16),
                   jax.ShapeDtypeStruct((s, LANES), F32)],
        scratch_shapes=[pltpu.VMEM((bm, d), BF16)],
        compiler_params=_cparams("parallel", "arbitrary"),
        name="in_proj",
    )(x, g, mod, mod, w, wkr)


def _rope_kernel(pcol_ref, prow_ref, frow_ref, srow_ref, fcol_ref, scol_ref, cs_ref, cst_ref):
    ang = pcol_ref[...].astype(F32) * frow_ref[...]
    lane = lax.broadcasted_iota(jnp.int32, ang.shape, 1)
    cs_ref[...] = jnp.where(lane < QK_ROPE_DIM, jnp.cos(ang), jnp.sin(ang) * srow_ref[...])
    ang_t = fcol_ref[...] * prow_ref[...].astype(F32)
    row = lax.broadcasted_iota(jnp.int32, ang_t.shape, 0)
    cst_ref[...] = jnp.where(row < QK_ROPE_DIM, jnp.cos(ang_t), jnp.sin(ang_t) * scol_ref[...])


def _rope_call(pos):
    s = pos.shape[0]
    half = QK_ROPE_DIM // 2
    inv_freq = ROPE_THETA ** (-jnp.arange(0, QK_ROPE_DIM, 2, dtype=F32) / QK_ROPE_DIM)
    invf = jnp.tile(inv_freq, 4)
    sign = jnp.concatenate([jnp.ones((2 * half,), F32), -jnp.ones((half,), F32), jnp.ones((half,), F32)])
    ts = _blk(1024, s)
    const_row = pl.BlockSpec((1, LANES), lambda i: (0, 0))
    const_col = pl.BlockSpec((LANES, 1), lambda i: (0, 0))
    return pl.pallas_call(
        _rope_kernel,
        grid=(s // ts,),
        in_specs=[pl.BlockSpec((ts, 1), lambda i: (i, 0)), pl.BlockSpec((1, ts), lambda i: (0, i)),
                  const_row, const_row, const_col, const_col],
        out_specs=[pl.BlockSpec((ts, LANES), lambda i: (i, 0)), pl.BlockSpec((LANES, ts), lambda i: (0, i))],
        out_shape=[jax.ShapeDtypeStruct((s, LANES), F32), jax.ShapeDtypeStruct((LANES, s), F32)],
        compiler_params=_cparams("parallel"),
        name="rope_tab",
    )(pos[:, None], pos[None, :], invf[None, :], sign[None, :], invf[:, None], sign[:, None])


def _lru_kernel(xr_ref, gr_ref, cw_ref, cb_ref, wax_ref, ba_ref, bx_ref, lam_ref, o_ref,
                xbuf, a_scr, b_scr, carry_scr):
    t = pl.program_id(1)
    tlen = xr_ref.shape[0]
    ngrp = tlen // SUBLANES

    @pl.when(t == 0)
    def _():
        xbuf[0:SUBLANES, :] = jnp.zeros((SUBLANES, LANES), F32)
        carry_scr[...] = jnp.zeros((SUBLANES, LANES), F32)

    x = xr_ref[...].astype(F32)
    xbuf[SUBLANES:SUBLANES + tlen, :] = x
    cw = cw_ref[...]
    xc = cb_ref[...]
    for k in range(CONV_WIDTH - 1):
        off = SUBLANES - (CONV_WIDTH - 1) + k
        xc = xc + xbuf[pl.ds(off, tlen), :] * cw[k:k + 1, :]
    xc = xc + x * cw[CONV_WIDTH - 1:CONV_WIDTH, :]
    xbuf[0:SUBLANES, :] = xbuf[tlen:tlen + SUBLANES, :]

    ri = jnp.dot(xc.astype(BF16), wax_ref[0], preferred_element_type=F32)
    r = _sigmoid(ri[:, :LANES] + ba_ref[...])
    ig = _sigmoid(ri[:, LANES:] + bx_ref[...])
    lam = lam_ref[...]
    softplus_neg_lam = jnp.maximum(-lam, 0.0) + jnp.log1p(jnp.exp(-jnp.abs(lam)))
    log_a = (-LRU_C * r) * softplus_neg_lam
    a = jnp.exp(log_a)
    u = jnp.sqrt(-jnp.tanh(log_a) * (1.0 + a * a)) * (ig * xc)

    av = a.reshape(ngrp, SUBLANES, LANES)
    bv = u.reshape(ngrp, SUBLANES, LANES)
    row = lax.broadcasted_iota(jnp.int32, av.shape, 1)
    for sh in (1, 2, 4):
        valid = row >= sh
        bv = jnp.where(valid, av * pltpu.roll(bv, sh, axis=1) + bv, bv)
        av = jnp.where(valid, av * pltpu.roll(av, sh, axis=1), av)
    a_scr[...] = av
    b_scr[...] = bv

    def group(g, carry):
        h = a_scr[g] * carry + b_scr[g]
        b_scr[g] = h
        return jnp.broadcast_to(h[SUBLANES - 1:SUBLANES, :], (SUBLANES, LANES))

    carry_scr[...] = lax.fori_loop(0, ngrp, group, carry_scr[...], unroll=8)
    h = b_scr[...].reshape(tlen, LANES)

    gt = gr_ref[...].astype(F32)
    gelu = 0.5 * gt * (1.0 + jnp.tanh(0.7978845608028654 * (gt + 0.044715 * (gt * gt * gt))))
    o_ref[...] = (h * gelu).astype(o_ref.dtype)


def _lru_call(proj, conv_w, conv_b, wax, b_a, b_x, lam, width):
    s = proj.shape[0]
    nblk = width // LANES
    ts = _blk(512, s)
    vec = lambda: pl.BlockSpec((1, LANES), lambda c, t: (0, c))
    return pl.pallas_call(
        _lru_kernel,
        grid=(nblk, s // ts),
        in_specs=[pl.BlockSpec((ts, LANES), lambda c, t: (t, c)),
                  pl.BlockSpec((ts, LANES), lambda c, t: (t, nblk + c)),
                  pl.BlockSpec((CONV_WIDTH, LANES), lambda c, t: (0, c)),
                  vec(),
                  pl.BlockSpec((1, LANES, 2 * LANES), lambda c, t: (c, 0, 0)),
                  vec(), vec(), vec()],
        out_specs=pl.BlockSpec((ts, LANES), lambda c, t: (t, c)),
        out_shape=jax.ShapeDtypeStruct((s, width), BF16),
        scratch_shapes=[pltpu.VMEM((ts + SUBLANES, LANES), F32),
                        pltpu.VMEM((ts // SUBLANES, SUBLANES, LANES), F32),
                        pltpu.VMEM((ts // SUBLANES, SUBLANES, LANES), F32),
                        pltpu.VMEM((SUBLANES, LANES), F32)],
        compiler_params=_cparams("parallel", "arbitrary"),
        name="lru",
    )(proj, proj, conv_w, conv_b, wax, b_a, b_x, lam)


def _qkv_kernel(ql_ref, kvl_ref, kr_ref, cs_ref, cst_ref, gq_ref, gkv_ref, wqt_ref, wkt_ref, wvt_ref,
                qt_ref, k_ref, vt_ref, qn_scr, kvn_scr, krot_scr, *, scale):
    bm = ql_ref.shape[0]
    gq, gkv = gq_ref[...], gkv_ref[...]
    for r in range(0, bm, ROW_CHUNK):
        rows = pl.ds(r, ROW_CHUNK)
        qn_scr[rows, :] = _rms(ql_ref[rows, :].astype(F32), gq).astype(BF16)
        kvn_scr[rows, :] = _rms(kvl_ref[rows, :].astype(F32), gkv).astype(BF16)
        prod = kr_ref[rows, :] * cs_ref[rows, :]
        rot = prod + pltpu.roll(prod, QK_ROPE_DIM, axis=1)
        lane = lax.broadcasted_iota(jnp.int32, rot.shape, 1)
        krot_scr[rows, :] = jnp.where(lane < QK_ROPE_DIM, rot, 0.0).astype(BF16)

    def head(h, carry):
        qn, kvn = qn_scr[...], kvn_scr[...]
        qt = lax.dot_general(wqt_ref[h], qn, NT_DIMS, preferred_element_type=F32)
        qt_ref[h, :QK_NOPE_DIM, :] = (qt[:QK_NOPE_DIM] * scale).astype(BF16)
        pair = qt[QK_NOPE_DIM:] * cst_ref[...]
        rot_t = (pair[:QK_ROPE_DIM] + pair[QK_ROPE_DIM:]) * scale
        qt_ref[h, QK_NOPE_DIM:QK_NOPE_DIM + QK_ROPE_DIM, :] = rot_t.astype(BF16)
        qt_ref[h, QK_NOPE_DIM + QK_ROPE_DIM:, :] = jnp.zeros((QK_PAD_DIM - QK_NOPE_DIM - QK_ROPE_DIM, bm), BF16)
        kn = lax.dot_general(kvn, wkt_ref[h], NT_DIMS, preferred_element_type=F32)
        k_ref[h, :, :QK_NOPE_DIM] = kn.astype(BF16)
        k_ref[h, :, QK_NOPE_DIM:] = krot_scr[...]
        vt = lax.dot_general(wvt_ref[h], kvn, NT_DIMS, preferred_element_type=F32)
        vt_ref[h, 0] = vt.astype(BF16)
        return carry

    lax.fori_loop(0, qt_ref.shape[0], head, 0)


def _qkv_call(proj, kr, cs, cst, g_q, g_kv, wqt, wkt, wvt, q_off, kv_off, scale):
    s = proj.shape[0]
    heads, _, rq = wqt.shape
    rkv = wkt.shape[2]
    bm = _blk(ATTN_BLOCK, s)
    whole = lambda shape: pl.BlockSpec(shape, lambda i: (0,) * len(shape))
    return pl.pallas_call(
        functools.partial(_qkv_kernel, scale=scale),
        grid=(s // bm,),
        in_specs=[pl.BlockSpec((bm, rq), lambda i: (i, q_off // rq)),
                  pl.BlockSpec((bm, rkv), lambda i: (i, kv_off // rkv)),
                  pl.BlockSpec((bm, LANES), lambda i: (i, 0)),
                  pl.BlockSpec((bm, LANES), lambda i: (i, 0)),
                  pl.BlockSpec((LANES, bm), lambda i: (0, i)),
                  whole((1, rq)), whole((1, rkv)),
                  whole(wqt.shape), whole(wkt.shape), whole(wvt.shape)],
        out_specs=[pl.BlockSpec((heads, QK_PAD_DIM, bm), lambda i: (0, 0, i)),
                   pl.BlockSpec((heads, bm, QK_PAD_DIM), lambda i: (0, i, 0)),
                   pl.BlockSpec((heads, 1, V_HEAD_DIM, bm), lambda i: (0, i, 0, 0))],
        out_shape=[jax.ShapeDtypeStruct((heads, QK_PAD_DIM, s), BF16),
                   jax.ShapeDtypeStruct((heads, s, QK_PAD_DIM), BF16),
                   jax.ShapeDtypeStruct((heads, s // bm, V_HEAD_DIM, bm), BF16)],
        scratch_shapes=[pltpu.VMEM((bm, rq), BF16), pltpu.VMEM((bm, rkv), BF16),
                        pltpu.VMEM((bm, LANES), BF16)],
        compiler_params=_cparams("parallel"),
        name="qkv_proj",
    )(proj, proj, kr, cs, cst, g_q, g_kv, wqt, wkt, wvt)


def _attn_kernel(qt_ref, k_ref, vt_ref, *rest, blk, n_cast):
    cast_in, o_ref, cast_out = rest[:n_cast], rest[n_cast], rest[n_cast + 1:2 * n_cast + 1]
    sa_ref, sb_ref = rest[2 * n_cast + 1:]
    for src, dst in zip(cast_in, cast_out):
        dst[...] = src[...].astype(BF16)

    i = pl.program_id(1)
    qt = qt_ref[0]

    def scores(j, s_ref):
        rows = pl.ds(pl.multiple_of(j * blk, blk), blk)
        s_ref[...] = jnp.dot(k_ref[0, rows, :], qt, preferred_element_type=F32)

    def update(j, s_ref, carry, masked):
        m, l, acc = carry
        if masked:
            kc = lax.broadcasted_iota(jnp.int32, (blk, blk), 0) // CHUNK
            qc = lax.broadcasted_iota(jnp.int32, (blk, blk), 1) // CHUNK
            s_ref[...] = jnp.where(kc <= qc, s_ref[...], -jnp.inf)
        m_new = jnp.maximum(m, jnp.max(s_ref[...], axis=0, keepdims=True))
        alpha = jnp.exp2(m - m_new)
        p = jnp.exp2(s_ref[...] - m_new)
        l = alpha * l + jnp.sum(p, axis=0, keepdims=True)
        acc = alpha * acc + jnp.dot(vt_ref[0, j], p.astype(BF16), preferred_element_type=F32)
        return m_new, l, acc

    def step(j, s_cur, s_nxt, carry):
        scores(j + 1, s_nxt)
        return update(j, s_cur, carry, False)

    def pair(t, carry):
        return step(2 * t + 1, sb_ref, sa_ref, step(2 * t, sa_ref, sb_ref, carry))

    init = (jnp.full((1, blk), -jnp.inf, F32), jnp.zeros((1, blk), F32), jnp.zeros((V_HEAD_DIM, blk), F32))
    scores(0, sa_ref)
    carry = lax.fori_loop(0, i // 2, pair, init)
    _, l, acc = lax.cond(
        i % 2 == 1,
        lambda c: update(i, sb_ref, step(i - 1, sa_ref, sb_ref, c), True),
        lambda c: update(i, sa_ref, c, True),
        carry)
    o_ref[...] = jnp.transpose(acc / l).astype(o_ref.dtype)


def _attn_call(qt, k, vt, cast_weights):
    heads, _, s = qt.shape
    blk = vt.shape[3]
    assert blk % CHUNK == 0
    nq = s // blk
    cast = [_cast_rows_specs(w, heads * nq, lambda h, i: h * nq + i) for w in cast_weights]
    outs = pl.pallas_call(
        functools.partial(_attn_kernel, blk=blk, n_cast=len(cast)),
        grid=(heads, nq),
        in_specs=[pl.BlockSpec((1, QK_PAD_DIM, blk), lambda h, i: (h, 0, i)),
                  pl.BlockSpec((1, s, QK_PAD_DIM), lambda h, i: (h, 0, 0)),
                  pl.BlockSpec((1, nq, V_HEAD_DIM, blk), lambda h, i: (h, 0, 0, 0))]
                 + [c[0] for c in cast],
        out_specs=[pl.BlockSpec((blk, V_HEAD_DIM), lambda h, i: (i, h))] + [c[1] for c in cast],
        out_shape=[jax.ShapeDtypeStruct((s, heads * V_HEAD_DIM), BF16)] + [c[2] for c in cast],
        scratch_shapes=[pltpu.VMEM((blk, blk), F32), pltpu.VMEM((blk, blk), F32)],
        compiler_params=_cparams("parallel", "arbitrary"),
        name="attn",
    )(qt, k, vt, *cast_weights)
    return outs[0], outs[1:]


def _out_proj_kernel(yl_ref, ym_ref, gl_ref, gm_ref, w_ref, o_ref, lhs_ref):
    width = yl_ref.shape[1]

    @pl.when(pl.program_id(1) == 0)
    def _():
        gl, gm = gl_ref[...], gm_ref[...]
        for r in range(0, yl_ref.shape[0], ROW_CHUNK):
            rows = pl.ds(r, ROW_CHUNK)
            lhs_ref[rows, :width] = _rms(yl_ref[rows, :].astype(F32), gl).astype(BF16)
            lhs_ref[rows, width:] = _rms(ym_ref[rows, :].astype(F32), gm).astype(BF16)

    o_ref[...] = jnp.dot(lhs_ref[...], w_ref[...], preferred_element_type=F32)


def _out_proj_call(y_lru, y_mla, g_lru, g_mla, w):
    s, width = y_lru.shape
    wm = y_mla.shape[1]
    k, n = w.shape
    bm, bn = _blk(512, s), _blk(1024, n)
    return pl.pallas_call(
        _out_proj_kernel,
        grid=(s // bm, n // bn),
        in_specs=[pl.BlockSpec((bm, width), lambda i, j: (i, 0)),
                  pl.BlockSpec((bm, wm), lambda i, j: (i, 0)),
                  pl.BlockSpec((1, width), lambda i, j: (0, 0)),
                  pl.BlockSpec((1, wm), lambda i, j: (0, 0)),
                  pl.BlockSpec((k, bn), lambda i, j: (0, j))],
        out_specs=pl.BlockSpec((bm, bn), lambda i, j: (i, j)),
        out_shape=jax.ShapeDtypeStruct((s, n), F32),
        scratch_shapes=[pltpu.VMEM((bm, k), BF16)],
        compiler_params=_cparams("parallel", "arbitrary"),
        name="out_proj",
    )(y_lru, y_mla, g_lru, g_mla, w)


def _post_mix_kernel(x_ref, y_ref, gpost_ref, gpre_ref, gt_ref, sh_ref, sc_ref, x1_ref, h_ref):
    x1 = x_ref[...] + gt_ref[...] * _rms(y_ref[...], gpost_ref[...])
    x1_ref[...] = x1
    h_ref[...] = (_rms(x1, gpre_ref[...]) * (1.0 + sc_ref[...]) + sh_ref[...]).astype(BF16)


def _post_mix_call(x, y, g_post, g_pre, mod):
    s, d = x.shape
    bm = _blk(256, s)
    row = pl.BlockSpec((bm, d), lambda i: (i, 0))
    chunk = lambda c: pl.BlockSpec((1, d), lambda i: (0, c))
    return pl.pallas_call(
        _post_mix_kernel,
        grid=(s // bm,),
        in_specs=[row, row, chunk(0), chunk(0), chunk(2), chunk(3), chunk(4)],
        out_specs=[row, row],
        out_shape=[jax.ShapeDtypeStruct((s, d), F32), jax.ShapeDtypeStruct((s, d), BF16)],
        compiler_params=_cparams("parallel"),
        name="post_mix",
    )(x, y, g_post, g_pre, mod, mod, mod)


def _gate_up_kernel(h_ref, wg_ref, wu_ref, wd_ref, o_ref, wd_out_ref):
    wd_out_ref[...] = wd_ref[...].astype(BF16)
    h = h_ref[...]
    g = jnp.dot(h, wg_ref[...], preferred_element_type=F32)
    u = jnp.dot(h, wu_ref[...], preferred_element_type=F32)
    o_ref[...] = (_silu(g) * u).astype(o_ref.dtype)


def _gate_up_call(h, wg, wu, w_down):
    s, d = h.shape
    f = wg.shape[1]
    bm, bn = _blk(2048, s), _blk(MXU_DIM, f)
    nj = f // bn
    wd_in, wd_out, wd_shape = _cast_rows_specs(w_down, (s // bm) * nj, lambda i, j: i * nj + j)
    return pl.pallas_call(
        _gate_up_kernel,
        grid=(s // bm, nj),
        in_specs=[pl.BlockSpec((bm, d), lambda i, j: (i, 0)),
                  pl.BlockSpec((d, bn), lambda i, j: (0, j)),
                  pl.BlockSpec((d, bn), lambda i, j: (0, j)),
                  wd_in],
        out_specs=[pl.BlockSpec((bm, bn), lambda i, j: (i, j)), wd_out],
        out_shape=[jax.ShapeDtypeStruct((s, f), BF16), wd_shape],
        compiler_params=_cparams("parallel", "arbitrary"),
        name="gate_up",
    )(h, wg, wu, w_down)


def _down_kernel(a_ref, w_ref, o_ref):
    o_ref[...] = jnp.dot(a_ref[...], w_ref[...], preferred_element_type=F32)


def _down_call(a, w):
    s, f = a.shape
    n = w.shape[1]
    bm, bn = _blk(512, s), _blk(512, n)
    return pl.pallas_call(
        _down_kernel,
        grid=(s // bm, n // bn),
        in_specs=[pl.BlockSpec((bm, f), lambda i, j: (i, 0)),
                  pl.BlockSpec((f, bn), lambda i, j: (0, j))],
        out_specs=pl.BlockSpec((bm, bn), lambda i, j: (i, j)),
        out_shape=jax.ShapeDtypeStruct((s, n), F32),
        compiler_params=_cparams("parallel", "arbitrary"),
        name="down",
    )(a, w)


def _final_kernel(x_ref, f_ref, g_ref, gt_ref, o_ref):
    o_ref[...] = x_ref[...] + gt_ref[...] * _rms(f_ref[...], g_ref[...])


def _final_call(x1, f, g, mod):
    s, d = x1.shape
    bm = _blk(256, s)
    row = pl.BlockSpec((bm, d), lambda i: (i, 0))
    return pl.pallas_call(
        _final_kernel,
        grid=(s // bm,),
        in_specs=[row, row, pl.BlockSpec((1, d), lambda i: (0, 0)),
                  pl.BlockSpec((1, d), lambda i: (0, 5))],
        out_specs=row,
        out_shape=jax.ShapeDtypeStruct((s, d), F32),
        compiler_params=_cparams("parallel"),
        name="final",
    )(x1, f, g, mod)


def _swap_halves(w, axis):
    lo, hi = jnp.split(w, 2, axis=axis)
    return jnp.concatenate([hi, lo], axis=axis)


def _layer(x, c_col, cs, cst, p):
    s, d = x.shape
    width = p["conv_w"].shape[-1]
    rq = p["g_q_lat"].shape[-1]
    rkv = p["g_kv_lat"].shape[-1]
    heads = p["w_kv_up"].shape[-1] // (QK_NOPE_DIM + V_HEAD_DIM)
    qk_dim = QK_NOPE_DIM + QK_ROPE_DIM

    mod = _mod_call(c_col, p["w_mod"], p["b_mod"][None, :])

    w_in = p["w_in"]
    n_main = 2 * width + rq + rkv
    w_kr = w_in[:, n_main:]
    w_kr = jnp.concatenate([w_kr, _swap_halves(w_kr, 1)], axis=1).astype(BF16)
    proj, kr = _in_proj_call(x, p["g_pre_mix"][None, :], mod, w_in.astype(BF16), w_kr, n_main)

    wax = jnp.concatenate([p["lru_w_a"], p["lru_w_x"]], axis=-1).astype(BF16)
    y_lru = _lru_call(proj, p["conv_w"], p["conv_b"][None, :], wax, p["lru_b_a"][None, :],
                      p["lru_b_x"][None, :], p["lru_lambda"][None, :], width)

    wq_t = p["w_q_up"].T.reshape(heads, qk_dim, rq)
    wq_t = jnp.concatenate([wq_t, _swap_halves(wq_t[:, QK_NOPE_DIM:], 1)], axis=1).astype(BF16)
    wkv_t = p["w_kv_up"].T.reshape(heads, QK_NOPE_DIM + V_HEAD_DIM, rkv).astype(BF16)
    scale = float(qk_dim) ** -0.5 * math.log2(math.e)
    qt, k, vt = _qkv_call(proj, kr, cs, cst, p["g_q_lat"][None, :], p["g_kv_lat"][None, :], wq_t,
                          wkv_t[:, :QK_NOPE_DIM], wkv_t[:, QK_NOPE_DIM:], 2 * width, 2 * width + rq, scale)
    y_mla, (w_out, w_gate, w_up) = _attn_call(qt, k, vt, [p["w_out"], p["w_gate"], p["w_up"]])

    y = _out_proj_call(y_lru, y_mla, p["g_lru_out"][None, :], p["g_mla_out"][None, :], w_out)
    x1, h2 = _post_mix_call(x, y, p["g_post_mix"][None, :], p["g_pre_ffn"][None, :], mod)

    act, w_down = _gate_up_call(h2, w_gate, w_up, p["w_down"])
    f = _down_call(act, w_down)
    return _final_call(x1, f, p["g_post_ffn"][None, :], mod)


def kernel(x, c, positions, w_mod, b_mod, g_pre_mix, w_in, conv_w, conv_b, lru_w_a, lru_b_a, lru_w_x,
           lru_b_x, lru_lambda, g_q_lat, w_q_up, g_kv_lat, w_kv_up, g_lru_out, g_mla_out, w_out,
           g_post_mix, g_pre_ffn, w_gate, w_up, w_down, g_post_ffn):
    params = dict(w_mod=w_mod, b_mod=b_mod, g_pre_mix=g_pre_mix, w_in=w_in, conv_w=conv_w, conv_b=conv_b,
                  lru_w_a=lru_w_a, lru_b_a=lru_b_a, lru_w_x=lru_w_x, lru_b_x=lru_b_x, lru_lambda=lru_lambda,
                  g_q_lat=g_q_lat, w_q_up=w_q_up, g_kv_lat=g_kv_lat, w_kv_up=w_kv_up, g_lru_out=g_lru_out,
                  g_mla_out=g_mla_out, w_out=w_out, g_post_mix=g_post_mix, g_pre_ffn=g_pre_ffn,
                  w_gate=w_gate, w_up=w_up, w_down=w_down, g_post_ffn=g_post_ffn)
    batch, depth = x.shape[0], w_mod.shape[0]
    outs = []
    for b in range(batch):
        xb = x[b]
        c_col = c[b][:, None]
        cs, cst = _rope_call(positions[b])
        for l in range(depth):
            xb = _layer(xb, c_col, cs, cst, {name: w[l] for name, w in params.items()})
        outs.append(xb)
    return jnp.stack(outs, axis=0)
```

```python
import functools
import math

import jax
import jax.numpy as jnp
from jax import lax
from jax.experimental import pallas as pl
from jax.experimental.pallas import tpu as pltpu

F32 = jnp.float32
BF16 = jnp.bfloat16

EPS = 1e-6
CHUNK = 64
CONV_WIDTH = 4
LRU_C = 8.0
QK_NOPE_DIM = 128
QK_ROPE_DIM = 64
V_HEAD_DIM = 128
ROPE_THETA = 10000.0
N_MOD = 6

LANES = 128
SUBLANES = 8
BF16_ROWS = 16
MXU_DIM = 256
VMEM_LIMIT_BYTES = 56 * 1024 * 1024

QK_PAD_DIM = MXU_DIM
ROW_CHUNK = 128
ATTN_BLOCK = 512
ATTN_HEADS_PER_STEP = 2

NT_DIMS = (((1,), (1,)), ((), ()))


def _cparams(*sem):
    return pltpu.CompilerParams(dimension_semantics=sem, vmem_limit_bytes=VMEM_LIMIT_BYTES)


def _blk(pref, dim):
    b = min(pref, dim)
    while dim % b:
        b //= 2
    return b


def _rms(x, g):
    ms = jnp.mean(x * x, axis=-1, keepdims=True)
    return x * lax.rsqrt(ms + EPS) * g


def _sigmoid(x):
    return 0.5 * (1.0 + jnp.tanh(0.5 * x))


def _silu(x):
    return x * _sigmoid(x)


def _cast_rows_specs(w, n_steps, step_of):
    rows, cols = w.shape
    assert rows % n_steps == 0 and (rows // n_steps) % BF16_ROWS == 0, (w.shape, n_steps)
    slab = rows // n_steps
    spec = pl.BlockSpec((slab, cols), lambda *ids: (step_of(*ids), 0))
    return spec, spec, jax.ShapeDtypeStruct((rows, cols), BF16)


def _mod_kernel(c_ref, w_ref, b_ref, o_ref):
    k = pl.program_id(1)
    c = c_ref[...]
    s = c * jax.nn.sigmoid(c)
    part = jnp.sum(w_ref[...] * s, axis=0, keepdims=True)

    @pl.when(k == 0)
    def _():
        o_ref[...] = b_ref[...] + part

    @pl.when(k != 0)
    def _():
        o_ref[...] += part


def _mod_call(c_col, w_mod, b_mod):
    d, n = w_mod.shape
    tk, bn = _blk(1024, d), _blk(2048, n)
    return pl.pallas_call(
        _mod_kernel,
        grid=(n // bn, d // tk),
        in_specs=[pl.BlockSpec((tk, 1), lambda j, k: (k, 0)),
                  pl.BlockSpec((tk, bn), lambda j, k: (k, j)),
                  pl.BlockSpec((1, bn), lambda j, k: (0, j))],
        out_specs=pl.BlockSpec((1, bn), lambda j, k: (0, j)),
        out_shape=jax.ShapeDtypeStruct((1, n), F32),
        compiler_params=_cparams("parallel", "arbitrary"),
        name="mod",
    )(c_col, w_mod, b_mod)


def _pre_mix_kernel(x_ref, g_ref, sh_ref, sc_ref, wkr_ref, h_ref, kr_ref):
    h = (_rms(x_ref[...], g_ref[...]) * (1.0 + sc_ref[...]) + sh_ref[...]).astype(BF16)
    h_ref[...] = h
    kr_ref[...] = jnp.dot(h, wkr_ref[...], preferred_element_type=F32)


def _pre_mix_call(x, g, mod, wkr):
    s, d = x.shape
    bm = _blk(256, s)
    return pl.pallas_call(
        _pre_mix_kernel,
        grid=(s // bm,),
        in_specs=[pl.BlockSpec((bm, d), lambda i: (i, 0)),
                  pl.BlockSpec((1, d), lambda i: (0, 0)),
                  pl.BlockSpec((1, d), lambda i: (0, 0)),
                  pl.BlockSpec((1, d), lambda i: (0, 1)),
                  pl.BlockSpec((d, LANES), lambda i: (0, 0))],
        out_specs=[pl.BlockSpec((bm, d), lambda i: (i, 0)),
                   pl.BlockSpec((bm, LANES), lambda i: (i, 0))],
        out_shape=[jax.ShapeDtypeStruct((s, d), BF16),
                   jax.ShapeDtypeStruct((s, LANES), F32)],
        compiler_params=_cparams("parallel"),
        name="pre_mix",
    )(x, g, mod, mod, wkr)


def _in_proj_kernel(h_ref, w_ref, o_ref, wbf_ref):
    @pl.when(pl.program_id(1) == 0)
    def _():
        wbf_ref[...] = w_ref[...].astype(BF16)

    o_ref[...] = jnp.dot(h_ref[...], wbf_ref[...], preferred_element_type=F32).astype(o_ref.dtype)


def _in_proj_call(h, w, n_main):
    s, d = h.shape
    bm, bn = _blk(1024, s), _blk(512, n_main)
    return pl.pallas_call(
        _in_proj_kernel,
        grid=(n_main // bn, s // bm),
        in_specs=[pl.BlockSpec((bm, d), lambda j, i: (i, 0)),
                  pl.BlockSpec((d, bn), lambda j, i: (0, j))],
        out_specs=pl.BlockSpec((bm, bn), lambda j, i: (i, j)),
        out_shape=jax.ShapeDtypeStruct((s, n_main), BF16),
        scratch_shapes=[pltpu.VMEM((d, bn), BF16)],
        compiler_params=_cparams("parallel", "arbitrary"),
        name="in_proj",
    )(h, w)


def _rope_kernel(pcol_ref, prow_ref, frow_ref, srow_ref, fcol_ref, scol_ref, cs_ref, cst_ref):
    ang = pcol_ref[...].astype(F32) * frow_ref[...]
    lane = lax.broadcasted_iota(jnp.int32, ang.shape, 1)
    cs_ref[...] = jnp.where(lane < QK_ROPE_DIM, jnp.cos(ang), jnp.sin(ang) * srow_ref[...])
    ang_t = fcol_ref[...] * prow_ref[...].astype(F32)
    row = lax.broadcasted_iota(jnp.int32, ang_t.shape, 0)
    cst_ref[...] = jnp.where(row < QK_ROPE_DIM, jnp.cos(ang_t), jnp.sin(ang_t) * scol_ref[...])


def _rope_call(pos):
    s = pos.shape[0]
    half = QK_ROPE_DIM // 2
    inv_freq = ROPE_THETA ** (-jnp.arange(0, QK_ROPE_DIM, 2, dtype=F32) / QK_ROPE_DIM)
    invf = jnp.tile(inv_freq, 4)
    sign = jnp.concatenate([jnp.ones((2 * half,), F32), -jnp.ones((half,), F32), jnp.ones((half,), F32)])
    ts = _blk(1024, s)
    const_row = pl.BlockSpec((1, LANES), lambda i: (0, 0))
    const_col = pl.BlockSpec((LANES, 1), lambda i: (0, 0))
    return pl.pallas_call(
        _rope_kernel,
        grid=(s // ts,),
        in_specs=[pl.BlockSpec((ts, 1), lambda i: (i, 0)), pl.BlockSpec((1, ts), lambda i: (0, i)),
                  const_row, const_row, const_col, const_col],
        out_specs=[pl.BlockSpec((ts, LANES), lambda i: (i, 0)), pl.BlockSpec((LANES, ts), lambda i: (0, i))],
        out_shape=[jax.ShapeDtypeStruct((s, LANES), F32), jax.ShapeDtypeStruct((LANES, s), F32)],
        compiler_params=_cparams("parallel"),
        name="rope_tab",
    )(pos[:, None], pos[None, :], invf[None, :], sign[None, :], invf[:, None], sign[:, None])


def _lru_kernel(xr_ref, gr_ref, cw_ref, cb_ref, wax_ref, ba_ref, bx_ref, lam_ref, o_ref,
                xbuf, a_scr, b_scr, alast_scr, blast_scr, carry_scr):
    t = pl.program_id(1)
    tlen = xr_ref.shape[0]
    ngrp = tlen // SUBLANES

    @pl.when(t == 0)
    def _():
        xbuf[0:SUBLANES, :] = jnp.zeros((SUBLANES, LANES), F32)
        carry_scr[...] = jnp.zeros((SUBLANES, LANES), F32)

    x = xr_ref[...].astype(F32)
    xbuf[SUBLANES:SUBLANES + tlen, :] = x
    cw = cw_ref[...]
    xc = cb_ref[...]
    for k in range(CONV_WIDTH - 1):
        off = SUBLANES - (CONV_WIDTH - 1) + k
        xc = xc + xbuf[pl.ds(off, tlen), :] * cw[k:k + 1, :]
    xc = xc + x * cw[CONV_WIDTH - 1:CONV_WIDTH, :]
    xbuf[0:SUBLANES, :] = xbuf[tlen:tlen + SUBLANES, :]

    ri = jnp.dot(xc.astype(BF16), wax_ref[0], preferred_element_type=F32)
    r = _sigmoid(ri[:, :LANES] + ba_ref[...])
    ig = _sigmoid(ri[:, LANES:] + bx_ref[...])
    lam = lam_ref[...]
    softplus_neg_lam = jnp.maximum(-lam, 0.0) + jnp.log1p(jnp.exp(-jnp.abs(lam)))
    log_a = (-LRU_C * r) * softplus_neg_lam
    a = jnp.exp(log_a)
    u = jnp.sqrt(-jnp.tanh(log_a) * (1.0 + a * a)) * (ig * xc)

    av = a.reshape(ngrp, SUBLANES, LANES)
    bv = u.reshape(ngrp, SUBLANES, LANES)
    row = lax.broadcasted_iota(jnp.int32, av.shape, 1)
    for sh in (1, 2, 4):
        valid = row >= sh
        bv = jnp.where(valid, av * pltpu.roll(bv, sh, axis=1) + bv, bv)
        av = jnp.where(valid, av * pltpu.roll(av, sh, axis=1), av)
    a_scr[...] = av
    b_scr[...] = bv
    alast_scr[...] = jnp.broadcast_to(av[:, SUBLANES - 1:SUBLANES, :], av.shape)
    blast_scr[...] = jnp.broadcast_to(bv[:, SUBLANES - 1:SUBLANES, :], bv.shape)

    def group(g, carry):
        b_scr[g] = a_scr[g] * carry + b_scr[g]
        return alast_scr[g] * carry + blast_scr[g]

    carry_scr[...] = lax.fori_loop(0, ngrp, group, carry_scr[...], unroll=8)
    h = b_scr[...].reshape(tlen, LANES)

    gt = gr_ref[...].astype(F32)
    gelu = 0.5 * gt * (1.0 + jnp.tanh(0.7978845608028654 * (gt + 0.044715 * (gt * gt * gt))))
    o_ref[...] = (h * gelu).astype(o_ref.dtype)


def _lru_call(proj, conv_w, conv_b, wax, b_a, b_x, lam, width):
    s = proj.shape[0]
    nblk = width // LANES
    ts = _blk(2048, s)
    grp_scratch = pltpu.VMEM((ts // SUBLANES, SUBLANES, LANES), F32)
    vec = lambda: pl.BlockSpec((1, LANES), lambda c, t: (0, c))
    return pl.pallas_call(
        _lru_kernel,
        grid=(nblk, s // ts),
        in_specs=[pl.BlockSpec((ts, LANES), lambda c, t: (t, c)),
                  pl.BlockSpec((ts, LANES), lambda c, t: (t, nblk + c)),
                  pl.BlockSpec((CONV_WIDTH, LANES), lambda c, t: (0, c)),
                  vec(),
                  pl.BlockSpec((1, LANES, 2 * LANES), lambda c, t: (c, 0, 0)),
                  vec(), vec(), vec()],
        out_specs=pl.BlockSpec((ts, LANES), lambda c, t: (t, c)),
        out_shape=jax.ShapeDtypeStruct((s, width), BF16),
        scratch_shapes=[pltpu.VMEM((ts + SUBLANES, LANES), F32),
                        grp_scratch, grp_scratch, grp_scratch, grp_scratch,
                        pltpu.VMEM((SUBLANES, LANES), F32)],
        compiler_params=_cparams("parallel", "arbitrary"),
        name="lru",
    )(proj, proj, conv_w, conv_b, wax, b_a, b_x, lam)


def _qkv_kernel(ql_ref, kvl_ref, kr_ref, cs_ref, cst_ref, gq_ref, gkv_ref, wqt_ref, wkt_ref, wvt_ref,
                qt_ref, k_ref, vt_ref, qn_scr, kvn_scr, krot_scr, *, scale):
    bm = ql_ref.shape[0]
    gq, gkv = gq_ref[...], gkv_ref[...]
    for r in range(0, bm, ROW_CHUNK):
        rows = pl.ds(r, ROW_CHUNK)
        qn_scr[rows, :] = _rms(ql_ref[rows, :].astype(F32), gq).astype(BF16)
        kvn_scr[rows, :] = _rms(kvl_ref[rows, :].astype(F32), gkv).astype(BF16)
        prod = kr_ref[rows, :] * cs_ref[rows, :]
        rot = prod + pltpu.roll(prod, QK_ROPE_DIM, axis=1)
        lane = lax.broadcasted_iota(jnp.int32, rot.shape, 1)
        krot_scr[rows, :] = jnp.where(lane < QK_ROPE_DIM, rot, 0.0).astype(BF16)

    def head(h, carry):
        qn, kvn = qn_scr[...], kvn_scr[...]
        qt = lax.dot_general(wqt_ref[h], qn, NT_DIMS, preferred_element_type=F32)
        qt_ref[h, :QK_NOPE_DIM, :] = (qt[:QK_NOPE_DIM] * scale).astype(BF16)
        pair = qt[QK_NOPE_DIM:] * cst_ref[...]
        rot_t = (pair[:QK_ROPE_DIM] + pair[QK_ROPE_DIM:]) * scale
        qt_ref[h, QK_NOPE_DIM:QK_NOPE_DIM + QK_ROPE_DIM, :] = rot_t.astype(BF16)
        qt_ref[h, QK_NOPE_DIM + QK_ROPE_DIM:, :] = jnp.zeros((QK_PAD_DIM - QK_NOPE_DIM - QK_ROPE_DIM, bm), BF16)
        kn = lax.dot_general(kvn, wkt_ref[h], NT_DIMS, preferred_element_type=F32)
        k_ref[h, :, :QK_NOPE_DIM] = kn.astype(BF16)
        k_ref[h, :, QK_NOPE_DIM:] = krot_scr[...]
        vt = lax.dot_general(wvt_ref[h], kvn, NT_DIMS, preferred_element_type=F32)
        vt_ref[h, 0] = vt.astype(BF16)
        return carry

    lax.fori_loop(0, qt_ref.shape[0], head, 0)


def _qkv_call(proj, kr, cs, cst, g_q, g_kv, wqt, wkt, wvt, q_off, kv_off, scale):
    s = proj.shape[0]
    heads, _, rq = wqt.shape
    rkv = wkt.shape[2]
    bm = _blk(ATTN_BLOCK, s)
    whole = lambda shape: pl.BlockSpec(shape, lambda i: (0,) * len(shape))
    return pl.pallas_call(
        functools.partial(_qkv_kernel, scale=scale),
        grid=(s // bm,),
        in_specs=[pl.BlockSpec((bm, rq), lambda i: (i, q_off // rq)),
                  pl.BlockSpec((bm, rkv), lambda i: (i, kv_off // rkv)),
                  pl.BlockSpec((bm, LANES), lambda i: (i, 0)),
                  pl.BlockSpec((bm, LANES), lambda i: (i, 0)),
                  pl.BlockSpec((LANES, bm), lambda i: (0, i)),
                  whole((1, rq)), whole((1, rkv)),
                  whole(wqt.shape), whole(wkt.shape), whole(wvt.shape)],
        out_specs=[pl.BlockSpec((heads, QK_PAD_DIM, bm), lambda i: (0, 0, i)),
                   pl.BlockSpec((heads, bm, QK_PAD_DIM), lambda i: (0, i, 0)),
                   pl.BlockSpec((heads, 1, V_HEAD_DIM, bm), lambda i: (0, i, 0, 0))],
        out_shape=[jax.ShapeDtypeStruct((heads, QK_PAD_DIM, s), BF16),
                   jax.ShapeDtypeStruct((heads, s, QK_PAD_DIM), BF16),
                   jax.ShapeDtypeStruct((heads, s // bm, V_HEAD_DIM, bm), BF16)],
        scratch_shapes=[pltpu.VMEM((bm, rq), BF16), pltpu.VMEM((bm, rkv), BF16),
                        pltpu.VMEM((bm, LANES), BF16)],
        compiler_params=_cparams("parallel"),
        name="qkv_proj",
    )(proj, proj, kr, cs, cst, g_q, g_kv, wqt, wkt, wvt)


def _attn_kernel(qt_ref, k_ref, vt_ref, *rest, blk, n_cast):
    cast_in, o_ref, cast_out = rest[:n_cast], rest[n_cast], rest[n_cast + 1:2 * n_cast + 1]
    sa_ref, sb_ref, smaxa_ref, smaxb_ref, m_ref, l_ref, acc_ref = rest[2 * n_cast + 1:]
    for src, dst in zip(cast_in, cast_out):
        dst[...] = src[...].astype(BF16)

    i = pl.program_id(1)
    hpb = qt_ref.shape[0]

    def scores(j, buf):
        s_ref, smax_ref = buf
        rows = pl.ds(pl.multiple_of(j * blk, blk), blk)
        for c in range(hpb):
            s = jnp.dot(k_ref[c, rows, :], qt_ref[c], preferred_element_type=F32)
            s_ref[c] = s
            smax_ref[c] = jnp.max(s, axis=0, keepdims=True)

    def update(j, buf, masked):
        s_ref, smax_ref = buf
        for c in range(hpb):
            if masked:
                kc = lax.broadcasted_iota(jnp.int32, (blk, blk), 0) // CHUNK
                qc = lax.broadcasted_iota(jnp.int32, (blk, blk), 1) // CHUNK
                s = jnp.where(kc <= qc, s_ref[c], -jnp.inf)
                s_ref[c] = s
                smax = jnp.max(s, axis=0, keepdims=True)
            else:
                smax = smax_ref[c]
            m = m_ref[c]
            m_new = jnp.maximum(m, smax)
            alpha = jnp.exp2(m - m_new)
            p = jnp.exp2(s_ref[c] - m_new)
            m_ref[c] = m_new
            l_ref[c] = alpha * l_ref[c] + jnp.sum(p, axis=0, keepdims=True)
            acc_ref[c] = alpha * acc_ref[c] + jnp.dot(vt_ref[c, j], p.astype(BF16),
                                                      preferred_element_type=F32)

    def step(j, cur, nxt):
        scores(j + 1, nxt)
        update(j, cur, False)

    buf_a, buf_b = (sa_ref, smaxa_ref), (sb_ref, smaxb_ref)

    def pair(t, carry):
        step(2 * t, buf_a, buf_b)
        step(2 * t + 1, buf_b, buf_a)
        return carry

    m_ref[...] = jnp.full(m_ref.shape, -jnp.inf, F32)
    l_ref[...] = jnp.zeros(l_ref.shape, F32)
    acc_ref[...] = jnp.zeros(acc_ref.shape, F32)
    scores(0, buf_a)

    def quad(t, carry):
        return pair(2 * t + 1, pair(2 * t, carry))

    lax.fori_loop(0, i // 4, quad, 0)

    @pl.when(i % 4 >= 2)
    def _():
        pair(i // 4 * 2, 0)

    @pl.when(i % 2 == 1)
    def _():
        step(i - 1, buf_a, buf_b)
        update(i, buf_b, True)

    @pl.when(i % 2 == 0)
    def _():
        update(i, buf_a, True)

    for c in range(hpb):
        o_ref[:, c * V_HEAD_DIM:(c + 1) * V_HEAD_DIM] = jnp.transpose(acc_ref[c] / l_ref[c]).astype(o_ref.dtype)


def _attn_call(qt, k, vt, cast_weights):
    heads, _, s = qt.shape
    blk = vt.shape[3]
    assert blk % CHUNK == 0
    nq = s // blk
    hpb = ATTN_HEADS_PER_STEP
    assert heads % hpb == 0
    ng = heads // hpb
    cast = [_cast_rows_specs(w, ng * nq, lambda h, i: h * nq + i) for w in cast_weights]
    outs = pl.pallas_call(
        functools.partial(_attn_kernel, blk=blk, n_cast=len(cast)),
        grid=(ng, nq),
        in_specs=[pl.BlockSpec((hpb, QK_PAD_DIM, blk), lambda h, i: (h, 0, i)),
                  pl.BlockSpec((hpb, s, QK_PAD_DIM), lambda h, i: (h, 0, 0)),
                  pl.BlockSpec((hpb, nq, V_HEAD_DIM, blk), lambda h, i: (h, 0, 0, 0))]
                 + [c[0] for c in cast],
        out_specs=[pl.BlockSpec((blk, hpb * V_HEAD_DIM), lambda h, i: (i, h))] + [c[1] for c in cast],
        out_shape=[jax.ShapeDtypeStruct((s, heads * V_HEAD_DIM), BF16)] + [c[2] for c in cast],
        scratch_shapes=[pltpu.VMEM((hpb, blk, blk), F32), pltpu.VMEM((hpb, blk, blk), F32),
                        pltpu.VMEM((hpb, 1, blk), F32), pltpu.VMEM((hpb, 1, blk), F32),
                        pltpu.VMEM((hpb, 1, blk), F32), pltpu.VMEM((hpb, 1, blk), F32),
                        pltpu.VMEM((hpb, V_HEAD_DIM, blk), F32)],
        compiler_params=_cparams("parallel", "arbitrary"),
        name="attn",
    )(qt, k, vt, *cast_weights)
    return outs[0], outs[1:]


def _out_proj_kernel(yl_ref, ym_ref, gl_ref, gm_ref, w_ref, o_ref, lhs_ref):
    width = yl_ref.shape[1]

    @pl.when(pl.program_id(1) == 0)
    def _():
        gl, gm = gl_ref[...], gm_ref[...]
        for r in range(0, yl_ref.shape[0], ROW_CHUNK):
            rows = pl.ds(r, ROW_CHUNK)
            lhs_ref[rows, :width] = _rms(yl_ref[rows, :].astype(F32), gl).astype(BF16)
            lhs_ref[rows, width:] = _rms(ym_ref[rows, :].astype(F32), gm).astype(BF16)

    o_ref[...] = jnp.dot(lhs_ref[...], w_ref[...], preferred_element_type=F32).astype(o_ref.dtype)


def _out_proj_call(y_lru, y_mla, g_lru, g_mla, w):
    s, width = y_lru.shape
    wm = y_mla.shape[1]
    k, n = w.shape
    bm, bn = _blk(512, s), _blk(1024, n)
    return pl.pallas_call(
        _out_proj_kernel,
        grid=(s // bm, n // bn),
        in_specs=[pl.BlockSpec((bm, width), lambda i, j: (i, 0)),
                  pl.BlockSpec((bm, wm), lambda i, j: (i, 0)),
                  pl.BlockSpec((1, width), lambda i, j: (0, 0)),
                  pl.BlockSpec((1, wm), lambda i, j: (0, 0)),
                  pl.BlockSpec((k, bn), lambda i, j: (0, j))],
        out_specs=pl.BlockSpec((bm, bn), lambda i, j: (i, j)),
        out_shape=jax.ShapeDtypeStruct((s, n), BF16),
        scratch_shapes=[pltpu.VMEM((bm, k), BF16)],
        compiler_params=_cparams("parallel", "arbitrary"),
        name="out_proj",
    )(y_lru, y_mla, g_lru, g_mla, w)


def _post_mix_kernel(x_ref, y_ref, gpost_ref, gpre_ref, gt_ref, sh_ref, sc_ref, x1_ref, h_ref):
    x1 = x_ref[...] + gt_ref[...] * _rms(y_ref[...].astype(F32), gpost_ref[...])
    x1_ref[...] = x1
    h_ref[...] = (_rms(x1, gpre_ref[...]) * (1.0 + sc_ref[...]) + sh_ref[...]).astype(BF16)


def _post_mix_call(x, y, g_post, g_pre, mod):
    s, d = x.shape
    bm = _blk(256, s)
    row = pl.BlockSpec((bm, d), lambda i: (i, 0))
    chunk = lambda c: pl.BlockSpec((1, d), lambda i: (0, c))
    return pl.pallas_call(
        _post_mix_kernel,
        grid=(s // bm,),
        in_specs=[row, row, chunk(0), chunk(0), chunk(2), chunk(3), chunk(4)],
        out_specs=[row, row],
        out_shape=[jax.ShapeDtypeStruct((s, d), F32), jax.ShapeDtypeStruct((s, d), BF16)],
        compiler_params=_cparams("parallel"),
        name="post_mix",
    )(x, y, g_post, g_pre, mod, mod, mod)


def _gate_up_kernel(h_ref, wg_ref, wu_ref, wd_ref, o_ref, wd_out_ref):
    wd_out_ref[...] = wd_ref[...].astype(BF16)
    h = h_ref[...]
    g = jnp.dot(h, wg_ref[...], preferred_element_type=F32)
    u = jnp.dot(h, wu_ref[...], preferred_element_type=F32)
    o_ref[...] = (_silu(g) * u).astype(o_ref.dtype)


def _gate_up_call(h, wg, wu, w_down):
    s, d = h.shape
    f = wg.shape[1]
    bm, bn = _blk(2048, s), _blk(MXU_DIM, f)
    nj = f // bn
    wd_in, wd_out, wd_shape = _cast_rows_specs(w_down, (s // bm) * nj, lambda i, j: i * nj + j)
    return pl.pallas_call(
        _gate_up_kernel,
        grid=(s // bm, nj),
        in_specs=[pl.BlockSpec((bm, d), lambda i, j: (i, 0)),
                  pl.BlockSpec((d, bn), lambda i, j: (0, j)),
                  pl.BlockSpec((d, bn), lambda i, j: (0, j)),
                  wd_in],
        out_specs=[pl.BlockSpec((bm, bn), lambda i, j: (i, j)), wd_out],
        out_shape=[jax.ShapeDtypeStruct((s, f), BF16), wd_shape],
        compiler_params=_cparams("parallel", "arbitrary"),
        name="gate_up",
    )(h, wg, wu, w_down)


def _down_kernel(a_ref, w_ref, o_ref):
    o_ref[...] = jnp.dot(a_ref[...], w_ref[...], preferred_element_type=F32).astype(o_ref.dtype)


def _down_call(a, w):
    s, f = a.shape
    n = w.shape[1]
    bm, bn = _blk(512, s), _blk(512, n)
    return pl.pallas_call(
        _down_kernel,
        grid=(s // bm, n // bn),
        in_specs=[pl.BlockSpec((bm, f), lambda i, j: (i, 0)),
                  pl.BlockSpec((f, bn), lambda i, j: (0, j))],
        out_specs=pl.BlockSpec((bm, bn), lambda i, j: (i, j)),
        out_shape=jax.ShapeDtypeStruct((s, n), BF16),
        compiler_params=_cparams("parallel", "arbitrary"),
        name="down",
    )(a, w)


def _final_kernel(x_ref, f_ref, g_ref, gt_ref, o_ref):
    o_ref[...] = x_ref[...] + gt_ref[...] * _rms(f_ref[...].astype(F32), g_ref[...])


def _final_call(x1, f, g, mod):
    s, d = x1.shape
    bm = _blk(256, s)
    row = pl.BlockSpec((bm, d), lambda i: (i, 0))
    return pl.pallas_call(
        _final_kernel,
        grid=(s // bm,),
        in_specs=[row, row, pl.BlockSpec((1, d), lambda i: (0, 0)),
                  pl.BlockSpec((1, d), lambda i: (0, 5))],
        out_specs=row,
        out_shape=jax.ShapeDtypeStruct((s, d), F32),
        compiler_params=_cparams("parallel"),
        name="final",
    )(x1, f, g, mod)


def _swap_halves(w, axis):
    lo, hi = jnp.split(w, 2, axis=axis)
    return jnp.concatenate([hi, lo], axis=axis)


def _layer(x, c_col, cs, cst, p):
    s, d = x.shape
    width = p["conv_w"].shape[-1]
    rq = p["g_q_lat"].shape[-1]
    rkv = p["g_kv_lat"].shape[-1]
    heads = p["w_kv_up"].shape[-1] // (QK_NOPE_DIM + V_HEAD_DIM)
    qk_dim = QK_NOPE_DIM + QK_ROPE_DIM

    mod = _mod_call(c_col, p["w_mod"], p["b_mod"][None, :])

    w_in = p["w_in"]
    n_main = 2 * width + rq + rkv
    w_kr = w_in[:, n_main:]
    w_kr = jnp.concatenate([w_kr, _swap_halves(w_kr, 1)], axis=1).astype(BF16)
    h, kr = _pre_mix_call(x, p["g_pre_mix"][None, :], mod, w_kr)
    proj = _in_proj_call(h, w_in, n_main)

    wax = jnp.concatenate([p["lru_w_a"], p["lru_w_x"]], axis=-1).astype(BF16)
    y_lru = _lru_call(proj, p["conv_w"], p["conv_b"][None, :], wax, p["lru_b_a"][None, :],
                      p["lru_b_x"][None, :], p["lru_lambda"][None, :], width)

    wq_t = p["w_q_up"].T.reshape(heads, qk_dim, rq)
    wq_t = jnp.concatenate([wq_t, _swap_halves(wq_t[:, QK_NOPE_DIM:], 1)], axis=1).astype(BF16)
    wkv_t = p["w_kv_up"].T.reshape(heads, QK_NOPE_DIM + V_HEAD_DIM, rkv).astype(BF16)
    scale = float(qk_dim) ** -0.5 * math.log2(math.e)
    qt, k, vt = _qkv_call(proj, kr, cs, cst, p["g_q_lat"][None, :], p["g_kv_lat"][None, :], wq_t,
                          wkv_t[:, :QK_NOPE_DIM], wkv_t[:, QK_NOPE_DIM:], 2 * width, 2 * width + rq, scale)
    y_mla, (w_out, w_gate, w_up) = _attn_call(qt, k, vt, [p["w_out"], p["w_gate"], p["w_up"]])

    y = _out_proj_call(y_lru, y_mla, p["g_lru_out"][None, :], p["g_mla_out"][None, :], w_out)
    x1, h2 = _post_mix_call(x, y, p["g_post_mix"][None, :], p["g_pre_ffn"][None, :], mod)

    act, w_down = _gate_up_call(h2, w_gate, w_up, p["w_down"])
    f = _down_call(act, w_down)
    return _final_call(x1, f, p["g_post_ffn"][None, :], mod)


def kernel(x, c, positions, w_mod, b_mod, g_pre_mix, w_in, conv_w, conv_b, lru_w_a, lru_b_a, lru_w_x,
           lru_b_x, lru_lambda, g_q_lat, w_q_up, g_kv_lat, w_kv_up, g_lru_out, g_mla_out, w_out,
           g_post_mix, g_pre_ffn, w_gate, w_up, w_down, g_post_ffn):
    params = dict(w_mod=w_mod, b_mod=b_mod, g_pre_mix=g_pre_mix, w_in=w_in, conv_w=conv_w, conv_b=conv_b,
                  lru_w_a=lru_w_a, lru_b_a=lru_b_a, lru_w_x=lru_w_x, lru_b_x=lru_b_x, lru_lambda=lru_lambda,
                  g_q_lat=g_q_lat, w_q_up=w_q_up, g_kv_lat=g_kv_lat, w_kv_up=w_kv_up, g_lru_out=g_lru_out,
                  g_mla_out=g_mla_out, w_out=w_out, g_post_mix=g_post_mix, g_pre_ffn=g_pre_ffn,
                  w_gate=w_gate, w_up=w_up, w_down=w_down, g_post_ffn=g_post_ffn)
    batch, depth = x.shape[0], w_mod.shape[0]
    outs = []
    for b in range(batch):
        xb = x[b]
        c_col = c[b][:, None]
        cs, cst = _rope_call(positions[b])
        for l in range(depth):
            xb = _layer(xb, c_col, cs, cst, {name: w[l] for name, w in params.items()})
        outs.append(xb)
    return jnp.stack(outs, axis=0)
```

```python
import functools
import math

import jax
import jax.numpy as jnp
from jax import lax
from jax.experimental import pallas as pl
from jax.experimental.pallas import tpu as pltpu

F32 = jnp.float32
BF16 = jnp.bfloat16

EPS = 1e-6
CHUNK = 64
CONV_WIDTH = 4
LRU_C = 8.0
QK_NOPE_DIM = 128
QK_ROPE_DIM = 64
V_HEAD_DIM = 128
ROPE_THETA = 10000.0
N_MOD = 6

LANES = 128
SUBLANES = 8
BF16_ROWS = 16
MXU_DIM = 256
VMEM_LIMIT_BYTES = 56 * 1024 * 1024

QK_PAD_DIM = MXU_DIM
ROW_CHUNK = 128
ATTN_BLOCK = 512
ATTN_HEADS_PER_STEP = 2

NT_DIMS = (((1,), (1,)), ((), ()))


def _cparams(*sem):
    return pltpu.CompilerParams(dimension_semantics=sem, vmem_limit_bytes=VMEM_LIMIT_BYTES)


def _blk(pref, dim):
    b = min(pref, dim)
    while dim % b:
        b //= 2
    return b


def _rms(x, g):
    ms = jnp.mean(x * x, axis=-1, keepdims=True)
    return x * lax.rsqrt(ms + EPS) * g


def _sigmoid(x):
    return 0.5 * (1.0 + jnp.tanh(0.5 * x))


def _silu(x):
    return x * _sigmoid(x)


def _cast_rows_specs(w, n_steps, step_of):
    rows, cols = w.shape
    assert rows % n_steps == 0 and (rows // n_steps) % BF16_ROWS == 0, (w.shape, n_steps)
    slab = rows // n_steps
    spec = pl.BlockSpec((slab, cols), lambda *ids: (step_of(*ids), 0))
    return spec, spec, jax.ShapeDtypeStruct((rows, cols), BF16)


def _mod_kernel(c_ref, w_ref, b_ref, o_ref):
    k = pl.program_id(1)
    c = c_ref[...]
    s = c * jax.nn.sigmoid(c)
    part = jnp.sum(w_ref[...] * s, axis=0, keepdims=True)

    @pl.when(k == 0)
    def _():
        o_ref[...] = b_ref[...] + part

    @pl.when(k != 0)
    def _():
        o_ref[...] += part


def _mod_call(c_col, w_mod, b_mod):
    d, n = w_mod.shape
    tk, bn = _blk(1024, d), _blk(2048, n)
    return pl.pallas_call(
        _mod_kernel,
        grid=(n // bn, d // tk),
        in_specs=[pl.BlockSpec((tk, 1), lambda j, k: (k, 0)),
                  pl.BlockSpec((tk, bn), lambda j, k: (k, j)),
                  pl.BlockSpec((1, bn), lambda j, k: (0, j))],
        out_specs=pl.BlockSpec((1, bn), lambda j, k: (0, j)),
        out_shape=jax.ShapeDtypeStruct((1, n), F32),
        compiler_params=_cparams("parallel", "arbitrary"),
        name="mod",
    )(c_col, w_mod, b_mod)


def _pre_mix_kernel(x_ref, g_ref, sh_ref, sc_ref, wkr_ref, h_ref, kr_ref):
    h = (_rms(x_ref[...], g_ref[...]) * (1.0 + sc_ref[...]) + sh_ref[...]).astype(BF16)
    h_ref[...] = h
    w = wkr_ref[...]
    half = w.shape[0] // 2
    pair = jnp.concatenate([w, w[half:], w[:half]], axis=0).astype(BF16)
    kr_ref[...] = lax.dot_general(h, pair, NT_DIMS, preferred_element_type=F32)


def _pre_mix_call(x, g, mod, w_t, n_main):
    s, d = x.shape
    bm = _blk(256, s)
    assert n_main % QK_ROPE_DIM == 0 and w_t.shape[0] == n_main + QK_ROPE_DIM
    return pl.pallas_call(
        _pre_mix_kernel,
        grid=(s // bm,),
        in_specs=[pl.BlockSpec((bm, d), lambda i: (i, 0)),
                  pl.BlockSpec((1, d), lambda i: (0, 0)),
                  pl.BlockSpec((1, d), lambda i: (0, 0)),
                  pl.BlockSpec((1, d), lambda i: (0, 1)),
                  pl.BlockSpec((QK_ROPE_DIM, d), lambda i: (n_main // QK_ROPE_DIM, 0))],
        out_specs=[pl.BlockSpec((bm, d), lambda i: (i, 0)),
                   pl.BlockSpec((bm, LANES), lambda i: (i, 0))],
        out_shape=[jax.ShapeDtypeStruct((s, d), BF16),
                   jax.ShapeDtypeStruct((s, LANES), F32)],
        compiler_params=_cparams("parallel"),
        name="pre_mix",
    )(x, g, mod, mod, w_t)


def _in_proj_kernel(h_ref, w_ref, o_ref, wbf_ref):
    @pl.when(pl.program_id(1) == 0)
    def _():
        wbf_ref[...] = w_ref[...].astype(BF16)

    o_ref[...] = lax.dot_general(h_ref[...], wbf_ref[...], NT_DIMS,
                                 preferred_element_type=F32).astype(o_ref.dtype)


def _in_proj_call(h, w_t, n_main):
    s, d = h.shape
    bm, bn = _blk(1024, s), _blk(512, n_main)
    return pl.pallas_call(
        _in_proj_kernel,
        grid=(n_main // bn, s // bm),
        in_specs=[pl.BlockSpec((bm, d), lambda j, i: (i, 0)),
                  pl.BlockSpec((bn, d), lambda j, i: (j, 0))],
        out_specs=pl.BlockSpec((bm, bn), lambda j, i: (i, j)),
        out_shape=jax.ShapeDtypeStruct((s, n_main), BF16),
        scratch_shapes=[pltpu.VMEM((bn, d), BF16)],
        compiler_params=_cparams("parallel", "arbitrary"),
        name="in_proj",
    )(h, w_t)


def _rope_kernel(pos_ref, freq_ref, cs_ref, cst_ref):
    ang = freq_ref[...] * pos_ref[...].astype(F32)
    cos, sin = jnp.cos(ang), jnp.sin(ang)
    cst = jnp.concatenate([cos, cos, -sin, sin], axis=0)
    cst_ref[...] = cst
    cs_ref[...] = jnp.transpose(cst)


def _rope_call(pos):
    s = pos.shape[0]
    half = QK_ROPE_DIM // 2
    assert 4 * half == LANES
    inv_freq = ROPE_THETA ** (-jnp.arange(0, QK_ROPE_DIM, 2, dtype=F32) / QK_ROPE_DIM)
    ts = _blk(1024, s)
    return pl.pallas_call(
        _rope_kernel,
        grid=(s // ts,),
        in_specs=[pl.BlockSpec((1, ts), lambda i: (0, i)), pl.BlockSpec((half, 1), lambda i: (0, 0))],
        out_specs=[pl.BlockSpec((ts, LANES), lambda i: (i, 0)), pl.BlockSpec((LANES, ts), lambda i: (0, i))],
        out_shape=[jax.ShapeDtypeStruct((s, LANES), F32), jax.ShapeDtypeStruct((LANES, s), F32)],
        compiler_params=_cparams("parallel"),
        name="rope_tab",
    )(pos[None, :], inv_freq[:, None])


def _lru_kernel(xr_ref, gr_ref, cw_ref, cb_ref, wax_ref, ba_ref, bx_ref, lam_ref, o_ref,
                xbuf, a_scr, b_scr, alast_scr, blast_scr, carry_scr):
    t = pl.program_id(1)
    tlen = xr_ref.shape[0]
    ngrp = tlen // SUBLANES

    @pl.when(t == 0)
    def _():
        xbuf[0:SUBLANES, :] = jnp.zeros((SUBLANES, LANES), F32)
        carry_scr[...] = jnp.zeros((SUBLANES, LANES), F32)

    x = xr_ref[...].astype(F32)
    xbuf[SUBLANES:SUBLANES + tlen, :] = x
    cw = cw_ref[...]
    xc = cb_ref[...]
    for k in range(CONV_WIDTH - 1):
        off = SUBLANES - (CONV_WIDTH - 1) + k
        xc = xc + xbuf[pl.ds(off, tlen), :] * cw[k:k + 1, :]
    xc = xc + x * cw[CONV_WIDTH - 1:CONV_WIDTH, :]
    xbuf[0:SUBLANES, :] = xbuf[tlen:tlen + SUBLANES, :]

    ri = jnp.dot(xc.astype(BF16), wax_ref[0], preferred_element_type=F32)
    r = _sigmoid(ri[:, :LANES] + ba_ref[...])
    ig = _sigmoid(ri[:, LANES:] + bx_ref[...])
    lam = lam_ref[...]
    softplus_neg_lam = jnp.maximum(-lam, 0.0) + jnp.log1p(jnp.exp(-jnp.abs(lam)))
    log_a = (-LRU_C * r) * softplus_neg_lam
    a = jnp.exp(log_a)
    u = jnp.sqrt(-jnp.tanh(log_a) * (1.0 + a * a)) * (ig * xc)

    av = a.reshape(ngrp, SUBLANES, LANES)
    bv = u.reshape(ngrp, SUBLANES, LANES)
    row = lax.broadcasted_iota(jnp.int32, av.shape, 1)
    for sh in (1, 2, 4):
        valid = row >= sh
        bv = jnp.where(valid, av * pltpu.roll(bv, sh, axis=1) + bv, bv)
        av = jnp.where(valid, av * pltpu.roll(av, sh, axis=1), av)
    a_scr[...] = av
    b_scr[...] = bv
    alast_scr[...] = jnp.broadcast_to(av[:, SUBLANES - 1:SUBLANES, :], av.shape)
    blast_scr[...] = jnp.broadcast_to(bv[:, SUBLANES - 1:SUBLANES, :], bv.shape)

    def group(g, carry):
        b_scr[g] = a_scr[g] * carry + b_scr[g]
        return alast_scr[g] * carry + blast_scr[g]

    carry_scr[...] = lax.fori_loop(0, ngrp, group, carry_scr[...], unroll=8)
    h = b_scr[...].reshape(tlen, LANES)

    gt = gr_ref[...].astype(F32)
    gelu = 0.5 * gt * (1.0 + jnp.tanh(0.7978845608028654 * (gt + 0.044715 * (gt * gt * gt))))
    o_ref[...] = (h * gelu).astype(o_ref.dtype)


def _lru_call(proj, conv_w, conv_b, wax, b_a, b_x, lam, width):
    s = proj.shape[0]
    nblk = width // LANES
    ts = _blk(2048, s)
    grp_scratch = pltpu.VMEM((ts // SUBLANES, SUBLANES, LANES), F32)
    vec = lambda: pl.BlockSpec((1, LANES), lambda c, t: (0, c))
    return pl.pallas_call(
        _lru_kernel,
        grid=(nblk, s // ts),
        in_specs=[pl.BlockSpec((ts, LANES), lambda c, t: (t, c)),
                  pl.BlockSpec((ts, LANES), lambda c, t: (t, nblk + c)),
                  pl.BlockSpec((CONV_WIDTH, LANES), lambda c, t: (0, c)),
                  vec(),
                  pl.BlockSpec((1, LANES, 2 * LANES), lambda c, t: (c, 0, 0)),
                  vec(), vec(), vec()],
        out_specs=pl.BlockSpec((ts, LANES), lambda c, t: (t, c)),
        out_shape=jax.ShapeDtypeStruct((s, width), BF16),
        scratch_shapes=[pltpu.VMEM((ts + SUBLANES, LANES), F32),
                        grp_scratch, grp_scratch, grp_scratch, grp_scratch,
                        pltpu.VMEM((SUBLANES, LANES), F32)],
        compiler_params=_cparams("parallel", "arbitrary"),
        name="lru",
    )(proj, proj, conv_w, conv_b, wax, b_a, b_x, lam)


def _qkv_kernel(ql_ref, kvl_ref, kr_ref, cs_ref, cst_ref, gq_ref, gkv_ref, wqt_ref, wkt_ref, wvt_ref,
                qt_ref, k_ref, vt_ref, qn_scr, kvn_scr, krot_scr, *, scale):
    bm = ql_ref.shape[0]
    gq, gkv = gq_ref[...], gkv_ref[...]
    for r in range(0, bm, ROW_CHUNK):
        rows = pl.ds(r, ROW_CHUNK)
        qn_scr[rows, :] = _rms(ql_ref[rows, :].astype(F32), gq).astype(BF16)
        kvn_scr[rows, :] = _rms(kvl_ref[rows, :].astype(F32), gkv).astype(BF16)
        prod = kr_ref[rows, :] * cs_ref[rows, :]
        rot = prod + pltpu.roll(prod, QK_ROPE_DIM, axis=1)
        lane = lax.broadcasted_iota(jnp.int32, rot.shape, 1)
        krot_scr[rows, :] = jnp.where(lane < QK_ROPE_DIM, rot, 0.0).astype(BF16)

    def head_pair(t, carry):
        qn, kvn = qn_scr[...], kvn_scr[...]
        kn = lax.dot_general(kvn, wkt_ref[t], NT_DIMS, preferred_element_type=F32)
        for c in range(2):
            h = 2 * t + c
            qt = lax.dot_general(wqt_ref[h], qn, NT_DIMS, preferred_element_type=F32)
            qt_ref[h, :QK_NOPE_DIM, :] = (qt[:QK_NOPE_DIM] * scale).astype(BF16)
            pair = qt[QK_NOPE_DIM:] * cst_ref[...]
            rot_t = (pair[:QK_ROPE_DIM] + pair[QK_ROPE_DIM:]) * scale
            qt_ref[h, QK_NOPE_DIM:QK_NOPE_DIM + QK_ROPE_DIM, :] = rot_t.astype(BF16)
            qt_ref[h, QK_NOPE_DIM + QK_ROPE_DIM:, :] = jnp.zeros(
                (QK_PAD_DIM - QK_NOPE_DIM - QK_ROPE_DIM, bm), BF16)
            k_ref[h, :, :QK_NOPE_DIM] = kn[:, c * QK_NOPE_DIM:(c + 1) * QK_NOPE_DIM].astype(BF16)
            k_ref[h, :, QK_NOPE_DIM:] = krot_scr[...]
            vt = lax.dot_general(wvt_ref[h], kvn, NT_DIMS, preferred_element_type=F32)
            vt_ref[h, 0] = vt.astype(BF16)
        return carry

    lax.fori_loop(0, qt_ref.shape[0] // 2, head_pair, 0)


def _qkv_call(proj, kr, cs, cst, g_q, g_kv, wqt, wkt, wvt, q_off, kv_off, scale):
    s = proj.shape[0]
    heads, _, rq = wqt.shape
    rkv = wkt.shape[2]
    bm = _blk(ATTN_BLOCK, s)
    whole = lambda shape: pl.BlockSpec(shape, lambda i: (0,) * len(shape))
    return pl.pallas_call(
        functools.partial(_qkv_kernel, scale=scale),
        grid=(s // bm,),
        in_specs=[pl.BlockSpec((bm, rq), lambda i: (i, q_off // rq)),
                  pl.BlockSpec((bm, rkv), lambda i: (i, kv_off // rkv)),
                  pl.BlockSpec((bm, LANES), lambda i: (i, 0)),
                  pl.BlockSpec((bm, LANES), lambda i: (i, 0)),
                  pl.BlockSpec((LANES, bm), lambda i: (0, i)),
                  whole((1, rq)), whole((1, rkv)),
                  whole(wqt.shape), whole(wkt.shape), whole(wvt.shape)],
        out_specs=[pl.BlockSpec((heads, QK_PAD_DIM, bm), lambda i: (0, 0, i)),
                   pl.BlockSpec((heads, bm, QK_PAD_DIM), lambda i: (0, i, 0)),
                   pl.BlockSpec((heads, 1, V_HEAD_DIM, bm), lambda i: (0, i, 0, 0))],
        out_shape=[jax.ShapeDtypeStruct((heads, QK_PAD_DIM, s), BF16),
                   jax.ShapeDtypeStruct((heads, s, QK_PAD_DIM), BF16),
                   jax.ShapeDtypeStruct((heads, s // bm, V_HEAD_DIM, bm), BF16)],
        scratch_shapes=[pltpu.VMEM((bm, rq), BF16), pltpu.VMEM((bm, rkv), BF16),
                        pltpu.VMEM((bm, LANES), BF16)],
        compiler_params=_cparams("parallel"),
        name="qkv_proj",
    )(proj, proj, kr, cs, cst, g_q, g_kv, wqt, wkt, wvt)


def _attn_kernel(qt_ref, k_ref, vt_ref, *rest, blk, n_cast):
    cast_in, o_ref, cast_out = rest[:n_cast], rest[n_cast], rest[n_cast + 1:2 * n_cast + 1]
    sa_ref, sb_ref, smaxa_ref, smaxb_ref, m_ref, l_ref, acc_ref = rest[2 * n_cast + 1:]
    for src, dst in zip(cast_in, cast_out):
        dst[...] = src[...].astype(BF16)

    i = pl.program_id(1)
    hpb = qt_ref.shape[0]

    def scores(j, buf):
        s_ref, smax_ref = buf
        rows = pl.ds(pl.multiple_of(j * blk, blk), blk)
        for c in range(hpb):
            s = jnp.dot(k_ref[c, rows, :], qt_ref[c], preferred_element_type=F32)
            s_ref[c] = s
            smax_ref[c] = jnp.max(s, axis=0, keepdims=True)

    def update(j, buf, masked):
        s_ref, smax_ref = buf
        for c in range(hpb):
            if masked:
                kc = lax.broadcasted_iota(jnp.int32, (blk, blk), 0) // CHUNK
                qc = lax.broadcasted_iota(jnp.int32, (blk, blk), 1) // CHUNK
                s = jnp.where(kc <= qc, s_ref[c], -jnp.inf)
                s_ref[c] = s
                smax = jnp.max(s, axis=0, keepdims=True)
            else:
                smax = smax_ref[c]
            m = m_ref[c]
            m_new = jnp.maximum(m, smax)
            alpha = jnp.exp2(m - m_new)
            p = jnp.exp2(s_ref[c] - m_new)
            m_ref[c] = m_new
            l_ref[c] = alpha * l_ref[c] + jnp.sum(p, axis=0, keepdims=True)
            acc_ref[c] = alpha * acc_ref[c] + jnp.dot(vt_ref[c, j], p.astype(BF16),
                                                      preferred_element_type=F32)

    def step(j, cur, nxt):
        scores(j + 1, nxt)
        update(j, cur, False)

    buf_a, buf_b = (sa_ref, smaxa_ref), (sb_ref, smaxb_ref)

    def pair(t, carry):
        step(2 * t, buf_a, buf_b)
        step(2 * t + 1, buf_b, buf_a)
        return carry

    m_ref[...] = jnp.full(m_ref.shape, -jnp.inf, F32)
    l_ref[...] = jnp.zeros(l_ref.shape, F32)
    acc_ref[...] = jnp.zeros(acc_ref.shape, F32)
    scores(0, buf_a)

    def quad(t, carry):
        return pair(2 * t + 1, pair(2 * t, carry))

    lax.fori_loop(0, i // 4, quad, 0)

    @pl.when(i % 4 >= 2)
    def _():
        pair(i // 4 * 2, 0)

    @pl.when(i % 2 == 1)
    def _():
        step(i - 1, buf_a, buf_b)
        update(i, buf_b, True)

    @pl.when(i % 2 == 0)
    def _():
        update(i, buf_a, True)

    for c in range(hpb):
        o_ref[:, c * V_HEAD_DIM:(c + 1) * V_HEAD_DIM] = jnp.transpose(acc_ref[c] / l_ref[c]).astype(o_ref.dtype)


def _attn_call(qt, k, vt, cast_weights):
    heads, _, s = qt.shape
    blk = vt.shape[3]
    assert blk % CHUNK == 0
    nq = s // blk
    hpb = ATTN_HEADS_PER_STEP
    assert heads % hpb == 0
    ng = heads // hpb
    cast = [_cast_rows_specs(w, ng * nq, lambda h, i: h * nq + i) for w in cast_weights]
    outs = pl.pallas_call(
        functools.partial(_attn_kernel, blk=blk, n_cast=len(cast)),
        grid=(ng, nq),
        in_specs=[pl.BlockSpec((hpb, QK_PAD_DIM, blk), lambda h, i: (h, 0, i)),
                  pl.BlockSpec((hpb, s, QK_PAD_DIM), lambda h, i: (h, 0, 0)),
                  pl.BlockSpec((hpb, nq, V_HEAD_DIM, blk), lambda h, i: (h, 0, 0, 0))]
                 + [c[0] for c in cast],
        out_specs=[pl.BlockSpec((blk, hpb * V_HEAD_DIM), lambda h, i: (i, h))] + [c[1] for c in cast],
        out_shape=[jax.ShapeDtypeStruct((s, heads * V_HEAD_DIM), BF16)] + [c[2] for c in cast],
        scratch_shapes=[pltpu.VMEM((hpb, blk, blk), F32), pltpu.VMEM((hpb, blk, blk), F32),
                        pltpu.VMEM((hpb, 1, blk), F32), pltpu.VMEM((hpb, 1, blk), F32),
                        pltpu.VMEM((hpb, 1, blk), F32), pltpu.VMEM((hpb, 1, blk), F32),
                        pltpu.VMEM((hpb, V_HEAD_DIM, blk), F32)],
        compiler_params=_cparams("parallel", "arbitrary"),
        name="attn",
    )(qt, k, vt, *cast_weights)
    return outs[0], outs[1:]


def _out_proj_kernel(yl_ref, ym_ref, gl_ref, gm_ref, w_ref, o_ref, lhs_ref):
    width = yl_ref.shape[1]

    @pl.when(pl.program_id(1) == 0)
    def _():
        gl, gm = gl_ref[...], gm_ref[...]
        for r in range(0, yl_ref.shape[0], ROW_CHUNK):
            rows = pl.ds(r, ROW_CHUNK)
            lhs_ref[rows, :width] = _rms(yl_ref[rows, :].astype(F32), gl).astype(BF16)
            lhs_ref[rows, width:] = _rms(ym_ref[rows, :].astype(F32), gm).astype(BF16)

    o_ref[...] = jnp.dot(lhs_ref[...], w_ref[...], preferred_element_type=F32).astype(o_ref.dtype)


def _out_proj_call(y_lru, y_mla, g_lru, g_mla, w):
    s, width = y_lru.shape
    wm = y_mla.shape[1]
    k, n = w.shape
    bm, bn = _blk(512, s), _blk(1024, n)
    return pl.pallas_call(
        _out_proj_kernel,
        grid=(s // bm, n // bn),
        in_specs=[pl.BlockSpec((bm, width), lambda i, j: (i, 0)),
                  pl.BlockSpec((bm, wm), lambda i, j: (i, 0)),
                  pl.BlockSpec((1, width), lambda i, j: (0, 0)),
                  pl.BlockSpec((1, wm), lambda i, j: (0, 0)),
                  pl.BlockSpec((k, bn), lambda i, j: (0, j))],
        out_specs=pl.BlockSpec((bm, bn), lambda i, j: (i, j)),
        out_shape=jax.ShapeDtypeStruct((s, n), BF16),
        scratch_shapes=[pltpu.VMEM((bm, k), BF16)],
        compiler_params=_cparams("parallel", "arbitrary"),
        name="out_proj",
    )(y_lru, y_mla, g_lru, g_mla, w)


def _post_mix_kernel(x_ref, y_ref, gpost_ref, gpre_ref, gt_ref, sh_ref, sc_ref, x1_ref, h_ref):
    x1 = x_ref[...] + gt_ref[...] * _rms(y_ref[...].astype(F32), gpost_ref[...])
    x1_ref[...] = x1
    h_ref[...] = (_rms(x1, gpre_ref[...]) * (1.0 + sc_ref[...]) + sh_ref[...]).astype(BF16)


def _post_mix_call(x, y, g_post, g_pre, mod):
    s, d = x.shape
    bm = _blk(256, s)
    row = pl.BlockSpec((bm, d), lambda i: (i, 0))
    chunk = lambda c: pl.BlockSpec((1, d), lambda i: (0, c))
    return pl.pallas_call(
        _post_mix_kernel,
        grid=(s // bm,),
        in_specs=[row, row, chunk(0), chunk(0), chunk(2), chunk(3), chunk(4)],
        out_specs=[row, row],
        out_shape=[jax.ShapeDtypeStruct((s, d), F32), jax.ShapeDtypeStruct((s, d), BF16)],
        compiler_params=_cparams("parallel"),
        name="post_mix",
    )(x, y, g_post, g_pre, mod, mod, mod)


def _gate_up_kernel(h_ref, wg_ref, wu_ref, wd_ref, o_ref, wd_out_ref):
    wd_out_ref[...] = wd_ref[...].astype(BF16)
    h = h_ref[...]
    g = jnp.dot(h, wg_ref[...], preferred_element_type=F32)
    u = jnp.dot(h, wu_ref[...], preferred_element_type=F32)
    o_ref[...] = (_silu(g) * u).astype(o_ref.dtype)


def _gate_up_call(h, wg, wu, w_down):
    s, d = h.shape
    f = wg.shape[1]
    bm, bn = _blk(2048, s), _blk(MXU_DIM, f)
    nj = f // bn
    wd_in, wd_out, wd_shape = _cast_rows_specs(w_down, (s // bm) * nj, lambda i, j: i * nj + j)
    return pl.pallas_call(
        _gate_up_kernel,
        grid=(s // bm, nj),
        in_specs=[pl.BlockSpec((bm, d), lambda i, j: (i, 0)),
                  pl.BlockSpec((d, bn), lambda i, j: (0, j)),
                  pl.BlockSpec((d, bn), lambda i, j: (0, j)),
                  wd_in],
        out_specs=[pl.BlockSpec((bm, bn), lambda i, j: (i, j)), wd_out],
        out_shape=[jax.ShapeDtypeStruct((s, f), BF16), wd_shape],
        compiler_params=_cparams("parallel", "arbitrary"),
        name="gate_up",
    )(h, wg, wu, w_down)


def _down_kernel(a_ref, w_ref, o_ref):
    o_ref[...] = jnp.dot(a_ref[...], w_ref[...], preferred_element_type=F32).astype(o_ref.dtype)


def _down_call(a, w):
    s, f = a.shape
    n = w.shape[1]
    bm, bn = _blk(512, s), _blk(512, n)
    return pl.pallas_call(
        _down_kernel,
        grid=(s // bm, n // bn),
        in_specs=[pl.BlockSpec((bm, f), lambda i, j: (i, 0)),
                  pl.BlockSpec((f, bn), lambda i, j: (0, j))],
        out_specs=pl.BlockSpec((bm, bn), lambda i, j: (i, j)),
        out_shape=jax.ShapeDtypeStruct((s, n), BF16),
        compiler_params=_cparams("parallel", "arbitrary"),
        name="down",
    )(a, w)


def _final_kernel(x_ref, f_ref, g_ref, gt_ref, o_ref):
    o_ref[...] = x_ref[...] + gt_ref[...] * _rms(f_ref[...].astype(F32), g_ref[...])


def _final_call(x1, f, g, mod):
    s, d = x1.shape
    bm = _blk(256, s)
    row = pl.BlockSpec((bm, d), lambda i: (i, 0))
    return pl.pallas_call(
        _final_kernel,
        grid=(s // bm,),
        in_specs=[row, row, pl.BlockSpec((1, d), lambda i: (0, 0)),
                  pl.BlockSpec((1, d), lambda i: (0, 5))],
        out_specs=row,
        out_shape=jax.ShapeDtypeStruct((s, d), F32),
        compiler_params=_cparams("parallel"),
        name="final",
    )(x1, f, g, mod)


def _swap_halves(w, axis):
    lo, hi = jnp.split(w, 2, axis=axis)
    return jnp.concatenate([hi, lo], axis=axis)


def _layer(x, c_col, cs, cst, p):
    s, d = x.shape
    width = p["conv_w"].shape[-1]
    rq = p["g_q_lat"].shape[-1]
    rkv = p["g_kv_lat"].shape[-1]
    heads = p["w_kv_up"].shape[-1] // (QK_NOPE_DIM + V_HEAD_DIM)
    qk_dim = QK_NOPE_DIM + QK_ROPE_DIM

    mod = _mod_call(c_col, p["w_mod"], p["b_mod"][None, :])

    w_in_t = p["w_in"].T
    n_main = 2 * width + rq + rkv
    h, kr = _pre_mix_call(x, p["g_pre_mix"][None, :], mod, w_in_t, n_main)
    proj = _in_proj_call(h, w_in_t, n_main)

    wax = jnp.concatenate([p["lru_w_a"], p["lru_w_x"]], axis=-1).astype(BF16)
    y_lru = _lru_call(proj, p["conv_w"], p["conv_b"][None, :], wax, p["lru_b_a"][None, :],
                      p["lru_b_x"][None, :], p["lru_lambda"][None, :], width)

    wq_t = p["w_q_up"].T.reshape(heads, qk_dim, rq)
    wq_t = jnp.concatenate([wq_t, _swap_halves(wq_t[:, QK_NOPE_DIM:], 1)], axis=1).astype(BF16)
    wkv_t = p["w_kv_up"].T.reshape(heads, QK_NOPE_DIM + V_HEAD_DIM, rkv).astype(BF16)
    scale = float(qk_dim) ** -0.5 * math.log2(math.e)
    qt, k, vt = _qkv_call(proj, kr, cs, cst, p["g_q_lat"][None, :], p["g_kv_lat"][None, :], wq_t,
                          wkv_t[:, :QK_NOPE_DIM].reshape(heads // 2, 2 * QK_NOPE_DIM, rkv),
                          wkv_t[:, QK_NOPE_DIM:], 2 * width, 2 * width + rq, scale)
    y_mla, (w_out, w_gate, w_up) = _attn_call(qt, k, vt, [p["w_out"], p["w_gate"], p["w_up"]])

    y = _out_proj_call(y_lru, y_mla, p["g_lru_out"][None, :], p["g_mla_out"][None, :], w_out)
    x1, h2 = _post_mix_call(x, y, p["g_post_mix"][None, :], p["g_pre_ffn"][None, :], mod)

    act, w_down = _gate_up_call(h2, w_gate, w_up, p["w_down"])
    f = _down_call(act, w_down)
    return _final_call(x1, f, p["g_post_ffn"][None, :], mod)


def kernel(x, c, positions, w_mod, b_mod, g_pre_mix, w_in, conv_w, conv_b, lru_w_a, lru_b_a, lru_w_x,
           lru_b_x, lru_lambda, g_q_lat, w_q_up, g_kv_lat, w_kv_up, g_lru_out, g_mla_out, w_out,
           g_post_mix, g_pre_ffn, w_gate, w_up, w_down, g_post_ffn):
    params = dict(w_mod=w_mod, b_mod=b_mod, g_pre_mix=g_pre_mix, w_in=w_in, conv_w=conv_w, conv_b=conv_b,
                  lru_w_a=lru_w_a, lru_b_a=lru_b_a, lru_w_x=lru_w_x, lru_b_x=lru_b_x, lru_lambda=lru_lambda,
                  g_q_lat=g_q_lat, w_q_up=w_q_up, g_kv_lat=g_kv_lat, w_kv_up=w_kv_up, g_lru_out=g_lru_out,
                  g_mla_out=g_mla_out, w_out=w_out, g_post_mix=g_post_mix, g_pre_ffn=g_pre_ffn,
                  w_gate=w_gate, w_up=w_up, w_down=w_down, g_post_ffn=g_post_ffn)
    batch, depth = x.shape[0], w_mod.shape[0]
    outs = []
    for b in range(batch):
        xb = x[b]
        c_col = c[b][:, None]
        cs, cst = _rope_call(positions[b])
        for l in range(depth):
            xb = _layer(xb, c_col, cs, cst, {name: w[l] for name, w in params.items()})
        outs.append(xb)
    return jnp.stack(outs, axis=0)
```

```python
import functools
import math

import jax
import jax.numpy as jnp
from jax import lax
from jax.experimental import pallas as pl
from jax.experimental.pallas import tpu as pltpu

F32 = jnp.float32
BF16 = jnp.bfloat16

EPS = 1e-6
CHUNK = 64
CONV_WIDTH = 4
LRU_C = 8.0
QK_NOPE_DIM = 128
QK_ROPE_DIM = 64
V_HEAD_DIM = 128
ROPE_THETA = 10000.0
N_MOD = 6

LANES = 128
SUBLANES = 8
BF16_ROWS = 16
MXU_DIM = 256
VMEM_LIMIT_BYTES = 56 * 1024 * 1024

QK_PAD_DIM = MXU_DIM
ROW_CHUNK = 128
ATTN_BLOCK = 512
ATTN_HEADS_PER_STEP = 2

NT_DIMS = (((1,), (1,)), ((), ()))


def _cparams(*sem):
    return pltpu.CompilerParams(dimension_semantics=sem, vmem_limit_bytes=VMEM_LIMIT_BYTES)


def _blk(pref, dim):
    b = min(pref, dim)
    while dim % b:
        b //= 2
    return b


def _rms(x, g):
    ms = jnp.mean(x * x, axis=-1, keepdims=True)
    return x * lax.rsqrt(ms + EPS) * g


def _sigmoid(x):
    return 0.5 * (1.0 + jnp.tanh(0.5 * x))


def _silu(x):
    return x * _sigmoid(x)


def _cast_rows_specs(w, n_steps, step_of):
    rows, cols = w.shape
    assert rows % n_steps == 0 and (rows // n_steps) % BF16_ROWS == 0, (w.shape, n_steps)
    slab = rows // n_steps
    spec = pl.BlockSpec((slab, cols), lambda *ids: (step_of(*ids), 0))
    return spec, spec, jax.ShapeDtypeStruct((rows, cols), BF16)


def _mod_kernel(c_ref, w_ref, b_ref, o_ref):
    k = pl.program_id(1)
    c = c_ref[...]
    s = c * jax.nn.sigmoid(c)
    part = jnp.sum(w_ref[...] * s, axis=0, keepdims=True)

    @pl.when(k == 0)
    def _():
        o_ref[...] = b_ref[...] + part

    @pl.when(k != 0)
    def _():
        o_ref[...] += part


def _mod_call(c_col, w_mod, b_mod, n):
    d = w_mod.shape[0]
    tk, bn = _blk(1024, d), _blk(2048, n)
    return pl.pallas_call(
        _mod_kernel,
        grid=(n // bn, d // tk),
        in_specs=[pl.BlockSpec((tk, 1), lambda j, k: (k, 0)),
                  pl.BlockSpec((tk, bn), lambda j, k: (k, j)),
                  pl.BlockSpec((1, bn), lambda j, k: (0, j))],
        out_specs=pl.BlockSpec((1, bn), lambda j, k: (0, j)),
        out_shape=jax.ShapeDtypeStruct((1, n), F32),
        compiler_params=_cparams("parallel", "arbitrary"),
        name="mod",
    )(c_col, w_mod, b_mod)


def _pre_mix_kernel(x_ref, g_ref, sh_ref, sc_ref, wkr_ref, h_ref, kr_ref):
    h = (_rms(x_ref[...], g_ref[...]) * (1.0 + sc_ref[...]) + sh_ref[...]).astype(BF16)
    h_ref[...] = h
    w = wkr_ref[...]
    half = w.shape[0] // 2
    pair = jnp.concatenate([w, w[half:], w[:half]], axis=0).astype(BF16)
    kr_ref[...] = lax.dot_general(h, pair, NT_DIMS, preferred_element_type=F32)


def _pre_mix_call(x, g, mod, w_t, n_main):
    s, d = x.shape
    bm = _blk(256, s)
    assert n_main % QK_ROPE_DIM == 0 and w_t.shape[0] == n_main + QK_ROPE_DIM
    return pl.pallas_call(
        _pre_mix_kernel,
        grid=(s // bm,),
        in_specs=[pl.BlockSpec((bm, d), lambda i: (i, 0)),
                  pl.BlockSpec((1, d), lambda i: (0, 0)),
                  pl.BlockSpec((1, d), lambda i: (0, 0)),
                  pl.BlockSpec((1, d), lambda i: (0, 1)),
                  pl.BlockSpec((QK_ROPE_DIM, d), lambda i: (n_main // QK_ROPE_DIM, 0))],
        out_specs=[pl.BlockSpec((bm, d), lambda i: (i, 0)),
                   pl.BlockSpec((bm, LANES), lambda i: (i, 0))],
        out_shape=[jax.ShapeDtypeStruct((s, d), BF16),
                   jax.ShapeDtypeStruct((s, LANES), F32)],
        compiler_params=_cparams("parallel"),
        name="pre_mix",
    )(x, g, mod, mod, w_t)


def _in_proj_kernel(h_ref, w_ref, o_ref, wbf_ref):
    @pl.when(pl.program_id(1) == 0)
    def _():
        wbf_ref[...] = w_ref[...].astype(BF16)

    o_ref[...] = lax.dot_general(h_ref[...], wbf_ref[...], NT_DIMS,
                                 preferred_element_type=F32).astype(o_ref.dtype)


def _in_proj_call(h, w_t, n_main):
    s, d = h.shape
    bm, bn = _blk(1024, s), _blk(512, n_main)
    return pl.pallas_call(
        _in_proj_kernel,
        grid=(n_main // bn, s // bm),
        in_specs=[pl.BlockSpec((bm, d), lambda j, i: (i, 0)),
                  pl.BlockSpec((bn, d), lambda j, i: (j, 0))],
        out_specs=pl.BlockSpec((bm, bn), lambda j, i: (i, j)),
        out_shape=jax.ShapeDtypeStruct((s, n_main), BF16),
        scratch_shapes=[pltpu.VMEM((bn, d), BF16)],
        compiler_params=_cparams("parallel", "arbitrary"),
        name="in_proj",
    )(h, w_t)


def _rope_kernel(pos_ref, freq_ref, cs_ref, cst_ref):
    ang = freq_ref[...] * pos_ref[...].astype(F32)
    cos, sin = jnp.cos(ang), jnp.sin(ang)
    cst = jnp.concatenate([cos, cos, -sin, sin], axis=0)
    cst_ref[...] = cst
    cs_ref[...] = jnp.transpose(cst)


def _rope_call(pos):
    s = pos.shape[0]
    half = QK_ROPE_DIM // 2
    assert 4 * half == LANES
    inv_freq = ROPE_THETA ** (-jnp.arange(0, QK_ROPE_DIM, 2, dtype=F32) / QK_ROPE_DIM)
    ts = _blk(1024, s)
    return pl.pallas_call(
        _rope_kernel,
        grid=(s // ts,),
        in_specs=[pl.BlockSpec((1, ts), lambda i: (0, i)), pl.BlockSpec((half, 1), lambda i: (0, 0))],
        out_specs=[pl.BlockSpec((ts, LANES), lambda i: (i, 0)), pl.BlockSpec((LANES, ts), lambda i: (0, i))],
        out_shape=[jax.ShapeDtypeStruct((s, LANES), F32), jax.ShapeDtypeStruct((LANES, s), F32)],
        compiler_params=_cparams("parallel"),
        name="rope_tab",
    )(pos[None, :], inv_freq[:, None])


def _lru_kernel(xr_ref, gr_ref, cw_ref, cb_ref, wax_ref, ba_ref, bx_ref, lam_ref, o_ref,
                xbuf, a_scr, b_scr, alast_scr, blast_scr, carry_scr):
    t = pl.program_id(1)
    tlen = xr_ref.shape[0]
    ngrp = tlen // SUBLANES

    @pl.when(t == 0)
    def _():
        xbuf[0:SUBLANES, :] = jnp.zeros((SUBLANES, LANES), F32)
        carry_scr[...] = jnp.zeros((SUBLANES, LANES), F32)

    x = xr_ref[...].astype(F32)
    xbuf[SUBLANES:SUBLANES + tlen, :] = x
    cw = cw_ref[...]
    xc = cb_ref[...]
    for k in range(CONV_WIDTH - 1):
        off = SUBLANES - (CONV_WIDTH - 1) + k
        xc = xc + xbuf[pl.ds(off, tlen), :] * cw[k:k + 1, :]
    xc = xc + x * cw[CONV_WIDTH - 1:CONV_WIDTH, :]
    xbuf[0:SUBLANES, :] = xbuf[tlen:tlen + SUBLANES, :]

    ri = jnp.dot(xc.astype(BF16), wax_ref[0], preferred_element_type=F32)
    r = _sigmoid(ri[:, :LANES] + ba_ref[...])
    ig = _sigmoid(ri[:, LANES:] + bx_ref[...])
    lam = lam_ref[...]
    softplus_neg_lam = jnp.maximum(-lam, 0.0) + jnp.log1p(jnp.exp(-jnp.abs(lam)))
    log_a = (-LRU_C * r) * softplus_neg_lam
    a = jnp.exp(log_a)
    u = jnp.sqrt(-jnp.tanh(log_a) * (1.0 + a * a)) * (ig * xc)

    av = a.reshape(ngrp, SUBLANES, LANES)
    bv = u.reshape(ngrp, SUBLANES, LANES)
    row = lax.broadcasted_iota(jnp.int32, av.shape, 1)
    for sh in (1, 2, 4):
        valid = row >= sh
        bv = jnp.where(valid, av * pltpu.roll(bv, sh, axis=1) + bv, bv)
        av = jnp.where(valid, av * pltpu.roll(av, sh, axis=1), av)
    a_scr[...] = av
    b_scr[...] = bv
    alast_scr[...] = jnp.broadcast_to(av[:, SUBLANES - 1:SUBLANES, :], av.shape)
    blast_scr[...] = jnp.broadcast_to(bv[:, SUBLANES - 1:SUBLANES, :], bv.shape)

    def group(g, carry):
        b_scr[g] = a_scr[g] * carry + b_scr[g]
        return alast_scr[g] * carry + blast_scr[g]

    carry_scr[...] = lax.fori_loop(0, ngrp, group, carry_scr[...], unroll=8)
    h = b_scr[...].reshape(tlen, LANES)

    gt = gr_ref[...].astype(F32)
    gelu = 0.5 * gt * (1.0 + jnp.tanh(0.7978845608028654 * (gt + 0.044715 * (gt * gt * gt))))
    o_ref[...] = (h * gelu).astype(o_ref.dtype)


def _lru_call(proj, conv_w, conv_b, wax, b_a, b_x, lam, width):
    s = proj.shape[0]
    nblk = width // LANES
    ts = _blk(2048, s)
    grp_scratch = pltpu.VMEM((ts // SUBLANES, SUBLANES, LANES), F32)
    vec = lambda: pl.BlockSpec((1, LANES), lambda c, t: (0, c))
    return pl.pallas_call(
        _lru_kernel,
        grid=(nblk, s // ts),
        in_specs=[pl.BlockSpec((ts, LANES), lambda c, t: (t, c)),
                  pl.BlockSpec((ts, LANES), lambda c, t: (t, nblk + c)),
                  pl.BlockSpec((CONV_WIDTH, LANES), lambda c, t: (0, c)),
                  vec(),
                  pl.BlockSpec((1, LANES, 2 * LANES), lambda c, t: (c, 0, 0)),
                  vec(), vec(), vec()],
        out_specs=pl.BlockSpec((ts, LANES), lambda c, t: (t, c)),
        out_shape=jax.ShapeDtypeStruct((s, width), BF16),
        scratch_shapes=[pltpu.VMEM((ts + SUBLANES, LANES), F32),
                        grp_scratch, grp_scratch, grp_scratch, grp_scratch,
                        pltpu.VMEM((SUBLANES, LANES), F32)],
        compiler_params=_cparams("parallel", "arbitrary"),
        name="lru",
    )(proj, proj, conv_w, conv_b, wax, b_a, b_x, lam)


def _qkv_kernel(ql_ref, kvl_ref, kr_ref, cs_ref, cst_ref, gq_ref, gkv_ref, wqt_ref, wkt_ref, wvt_ref,
                qt_ref, k_ref, vt_ref, qn_scr, kvn_scr, krot_scr, *, scale):
    bm = ql_ref.shape[0]
    gq, gkv = gq_ref[...], gkv_ref[...]
    for r in range(0, bm, ROW_CHUNK):
        rows = pl.ds(r, ROW_CHUNK)
        qn_scr[rows, :] = _rms(ql_ref[rows, :].astype(F32), gq).astype(BF16)
        kvn_scr[rows, :] = _rms(kvl_ref[rows, :].astype(F32), gkv).astype(BF16)
        prod = kr_ref[rows, :] * cs_ref[rows, :]
        rot = prod + pltpu.roll(prod, QK_ROPE_DIM, axis=1)
        lane = lax.broadcasted_iota(jnp.int32, rot.shape, 1)
        krot_scr[rows, :] = jnp.where(lane < QK_ROPE_DIM, rot, 0.0).astype(BF16)

    def head_pair(t, carry):
        qn, kvn = qn_scr[...], kvn_scr[...]
        kn = lax.dot_general(kvn, wkt_ref[t], NT_DIMS, preferred_element_type=F32)
        for c in range(2):
            h = 2 * t + c
            qt = lax.dot_general(wqt_ref[h], qn, NT_DIMS, preferred_element_type=F32)
            qt_ref[h, :QK_NOPE_DIM, :] = (qt[:QK_NOPE_DIM] * scale).astype(BF16)
            pair = qt[QK_NOPE_DIM:] * cst_ref[...]
            rot_t = (pair[:QK_ROPE_DIM] + pair[QK_ROPE_DIM:]) * scale
            qt_ref[h, QK_NOPE_DIM:QK_NOPE_DIM + QK_ROPE_DIM, :] = rot_t.astype(BF16)
            qt_ref[h, QK_NOPE_DIM + QK_ROPE_DIM:, :] = jnp.zeros(
                (QK_PAD_DIM - QK_NOPE_DIM - QK_ROPE_DIM, bm), BF16)
            k_ref[h, :, :QK_NOPE_DIM] = kn[:, c * QK_NOPE_DIM:(c + 1) * QK_NOPE_DIM].astype(BF16)
            k_ref[h, :, QK_NOPE_DIM:] = krot_scr[...]
            vt = lax.dot_general(wvt_ref[h], kvn, NT_DIMS, preferred_element_type=F32)
            vt_ref[h, 0] = vt.astype(BF16)
        return carry

    lax.fori_loop(0, qt_ref.shape[0] // 2, head_pair, 0)


def _qkv_call(proj, kr, cs, cst, g_q, g_kv, wqt, wkt, wvt, q_off, kv_off, scale):
    s = proj.shape[0]
    heads, _, rq = wqt.shape
    rkv = wkt.shape[2]
    bm = _blk(ATTN_BLOCK, s)
    whole = lambda shape: pl.BlockSpec(shape, lambda i: (0,) * len(shape))
    return pl.pallas_call(
        functools.partial(_qkv_kernel, scale=scale),
        grid=(s // bm,),
        in_specs=[pl.BlockSpec((bm, rq), lambda i: (i, q_off // rq)),
                  pl.BlockSpec((bm, rkv), lambda i: (i, kv_off // rkv)),
                  pl.BlockSpec((bm, LANES), lambda i: (i, 0)),
                  pl.BlockSpec((bm, LANES), lambda i: (i, 0)),
                  pl.BlockSpec((LANES, bm), lambda i: (0, i)),
                  whole((1, rq)), whole((1, rkv)),
                  whole(wqt.shape), whole(wkt.shape), whole(wvt.shape)],
        out_specs=[pl.BlockSpec((heads, QK_PAD_DIM, bm), lambda i: (0, 0, i)),
                   pl.BlockSpec((heads, bm, QK_PAD_DIM), lambda i: (0, i, 0)),
                   pl.BlockSpec((heads, 1, V_HEAD_DIM, bm), lambda i: (0, i, 0, 0))],
        out_shape=[jax.ShapeDtypeStruct((heads, QK_PAD_DIM, s), BF16),
                   jax.ShapeDtypeStruct((heads, s, QK_PAD_DIM), BF16),
                   jax.ShapeDtypeStruct((heads, s // bm, V_HEAD_DIM, bm), BF16)],
        scratch_shapes=[pltpu.VMEM((bm, rq), BF16), pltpu.VMEM((bm, rkv), BF16),
                        pltpu.VMEM((bm, LANES), BF16)],
        compiler_params=_cparams("parallel"),
        name="qkv_proj",
    )(proj, proj, kr, cs, cst, g_q, g_kv, wqt, wkt, wvt)


def _attn_kernel(qt_ref, k_ref, vt_ref, c_ref, wm_ref, bm_ref, *rest, blk, n_cast):
    cast_in, o_ref, mod_ref = rest[:n_cast], rest[n_cast], rest[n_cast + 1]
    cast_out = rest[n_cast + 2:2 * n_cast + 2]
    sa_ref, sb_ref, smaxa_ref, smaxb_ref, m_ref, l_ref, acc_ref, modpart_ref = rest[2 * n_cast + 2:]
    for src, dst in zip(cast_in, cast_out):
        dst[...] = src[...].astype(BF16)
    i = pl.program_id(1)
    c = c_ref[...]
    modpart_ref[...] = jnp.sum(wm_ref[...] * (c * jax.nn.sigmoid(c)), axis=0, keepdims=True)

    hpb = qt_ref.shape[0]

    def scores(j, buf):
        s_ref, smax_ref = buf
        rows = pl.ds(pl.multiple_of(j * blk, blk), blk)
        for c in range(hpb):
            s = jnp.dot(k_ref[c, rows, :], qt_ref[c], preferred_element_type=F32)
            s_ref[c] = s
            smax_ref[c] = jnp.max(s, axis=0, keepdims=True)

    def update(j, buf, masked):
        s_ref, smax_ref = buf
        for c in range(hpb):
            if masked:
                kc = lax.broadcasted_iota(jnp.int32, (blk, blk), 0) // CHUNK
                qc = lax.broadcasted_iota(jnp.int32, (blk, blk), 1) // CHUNK
                s = jnp.where(kc <= qc, s_ref[c], -jnp.inf)
                s_ref[c] = s
                smax = jnp.max(s, axis=0, keepdims=True)
            else:
                smax = smax_ref[c]
            m = m_ref[c]
            m_new = jnp.maximum(m, smax)
            alpha = jnp.exp2(m - m_new)
            p = jnp.exp2(s_ref[c] - m_new)
            m_ref[c] = m_new
            l_ref[c] = alpha * l_ref[c] + jnp.sum(p, axis=0, keepdims=True)
            acc_ref[c] = alpha * acc_ref[c] + jnp.dot(vt_ref[c, j], p.astype(BF16),
                                                      preferred_element_type=F32)

    def step(j, cur, nxt):
        scores(j + 1, nxt)
        update(j, cur, False)

    buf_a, buf_b = (sa_ref, smaxa_ref), (sb_ref, smaxb_ref)

    def pair(t, carry):
        step(2 * t, buf_a, buf_b)
        step(2 * t + 1, buf_b, buf_a)
        return carry

    m_ref[...] = jnp.full(m_ref.shape, -jnp.inf, F32)
    l_ref[...] = jnp.zeros(l_ref.shape, F32)
    acc_ref[...] = jnp.zeros(acc_ref.shape, F32)
    scores(0, buf_a)

    def quad(t, carry):
        return pair(2 * t + 1, pair(2 * t, carry))

    lax.fori_loop(0, i // 4, quad, 0)

    @pl.when(i % 4 >= 2)
    def _():
        pair(i // 4 * 2, 0)

    @pl.when(i % 2 == 1)
    def _():
        step(i - 1, buf_a, buf_b)
        update(i, buf_b, True)

    @pl.when(i % 2 == 0)
    def _():
        update(i, buf_a, True)

    for c in range(hpb):
        o_ref[:, c * V_HEAD_DIM:(c + 1) * V_HEAD_DIM] = jnp.transpose(acc_ref[c] / l_ref[c]).astype(o_ref.dtype)

    step_id = pl.program_id(0) * pl.num_programs(1) + i

    @pl.when(step_id == 0)
    def _():
        mod_ref[...] = bm_ref[...] + modpart_ref[...]

    @pl.when(step_id != 0)
    def _():
        mod_ref[...] += modpart_ref[...]


def _attn_call(qt, k, vt, cast_weights, c_col, w_mod, b_mod, n_mod_done):
    heads, _, s = qt.shape
    blk = vt.shape[3]
    assert blk % CHUNK == 0
    nq = s // blk
    hpb = ATTN_HEADS_PER_STEP
    assert heads % hpb == 0
    ng = heads // hpb
    step_of = lambda h, i: h * nq + i
    cast = [_cast_rows_specs(w, ng * nq, step_of) for w in cast_weights]
    d, n_mod = w_mod.shape
    assert n_mod == 2 * n_mod_done and d % (ng * nq) == 0 and (d // (ng * nq)) % SUBLANES == 0
    slab = d // (ng * nq)
    outs = pl.pallas_call(
        functools.partial(_attn_kernel, blk=blk, n_cast=len(cast)),
        grid=(ng, nq),
        in_specs=[pl.BlockSpec((hpb, QK_PAD_DIM, blk), lambda h, i: (h, 0, i)),
                  pl.BlockSpec((hpb, s, QK_PAD_DIM), lambda h, i: (h, 0, 0)),
                  pl.BlockSpec((hpb, nq, V_HEAD_DIM, blk), lambda h, i: (h, 0, 0, 0)),
                  pl.BlockSpec((slab, 1), lambda h, i: (step_of(h, i), 0)),
                  pl.BlockSpec((slab, n_mod_done), lambda h, i: (step_of(h, i), 1)),
                  pl.BlockSpec((1, n_mod_done), lambda h, i: (0, 1))]
                 + [c[0] for c in cast],
        out_specs=[pl.BlockSpec((blk, hpb * V_HEAD_DIM), lambda h, i: (i, h)),
                   pl.BlockSpec((1, n_mod_done), lambda h, i: (0, 0))] + [c[1] for c in cast],
        out_shape=[jax.ShapeDtypeStruct((s, heads * V_HEAD_DIM), BF16),
                   jax.ShapeDtypeStruct((1, n_mod_done), F32)] + [c[2] for c in cast],
        scratch_shapes=[pltpu.VMEM((hpb, blk, blk), F32), pltpu.VMEM((hpb, blk, blk), F32),
                        pltpu.VMEM((hpb, 1, blk), F32), pltpu.VMEM((hpb, 1, blk), F32),
                        pltpu.VMEM((hpb, 1, blk), F32), pltpu.VMEM((hpb, 1, blk), F32),
                        pltpu.VMEM((hpb, V_HEAD_DIM, blk), F32),
                        pltpu.VMEM((1, n_mod_done), F32)],
        compiler_params=_cparams("arbitrary", "arbitrary"),
        name="attn",
    )(qt, k, vt, c_col, w_mod, b_mod, *cast_weights)
    return outs[0], outs[1], outs[2:]


def _out_proj_kernel(yl_ref, ym_ref, gl_ref, gm_ref, w_ref, o_ref, lhs_ref):
    width = yl_ref.shape[1]

    @pl.when(pl.program_id(1) == 0)
    def _():
        gl, gm = gl_ref[...], gm_ref[...]
        for r in range(0, yl_ref.shape[0], ROW_CHUNK):
            rows = pl.ds(r, ROW_CHUNK)
            lhs_ref[rows, :width] = _rms(yl_ref[rows, :].astype(F32), gl).astype(BF16)
            lhs_ref[rows, width:] = _rms(ym_ref[rows, :].astype(F32), gm).astype(BF16)

    o_ref[...] = jnp.dot(lhs_ref[...], w_ref[...], preferred_element_type=F32).astype(o_ref.dtype)


def _out_proj_call(y_lru, y_mla, g_lru, g_mla, w):
    s, width = y_lru.shape
    wm = y_mla.shape[1]
    k, n = w.shape
    bm, bn = _blk(512, s), _blk(1024, n)
    return pl.pallas_call(
        _out_proj_kernel,
        grid=(s // bm, n // bn),
        in_specs=[pl.BlockSpec((bm, width), lambda i, j: (i, 0)),
                  pl.BlockSpec((bm, wm), lambda i, j: (i, 0)),
                  pl.BlockSpec((1, width), lambda i, j: (0, 0)),
                  pl.BlockSpec((1, wm), lambda i, j: (0, 0)),
                  pl.BlockSpec((k, bn), lambda i, j: (0, j))],
        out_specs=pl.BlockSpec((bm, bn), lambda i, j: (i, j)),
        out_shape=jax.ShapeDtypeStruct((s, n), BF16),
        scratch_shapes=[pltpu.VMEM((bm, k), BF16)],
        compiler_params=_cparams("parallel", "arbitrary"),
        name="out_proj",
    )(y_lru, y_mla, g_lru, g_mla, w)


def _mixed_residual(x_ref, y_ref, gpost_ref, gt_ref):
    return x_ref[...] + gt_ref[...] * _rms(y_ref[...].astype(F32), gpost_ref[...])


def _post_mix_kernel(x_ref, y_ref, gpost_ref, gpre_ref, gt_ref, sh_ref, sc_ref, h_ref):
    x1 = _mixed_residual(x_ref, y_ref, gpost_ref, gt_ref)
    h_ref[...] = (_rms(x1, gpre_ref[...]) * (1.0 + sc_ref[...]) + sh_ref[...]).astype(BF16)


def _post_mix_call(x, y, g_post, g_pre, mod_a, mod_f):
    s, d = x.shape
    bm = _blk(256, s)
    row = pl.BlockSpec((bm, d), lambda i: (i, 0))
    chunk = lambda c: pl.BlockSpec((1, d), lambda i: (0, c))
    return pl.pallas_call(
        _post_mix_kernel,
        grid=(s // bm,),
        in_specs=[row, row, chunk(0), chunk(0), chunk(2), chunk(0), chunk(1)],
        out_specs=row,
        out_shape=jax.ShapeDtypeStruct((s, d), BF16),
        compiler_params=_cparams("parallel"),
        name="post_mix",
    )(x, y, g_post, g_pre, mod_a, mod_f, mod_f)


def _gate_up_kernel(h_ref, wg_ref, wu_ref, wd_ref, o_ref, wd_out_ref):
    wd_out_ref[...] = wd_ref[...].astype(BF16)
    h = h_ref[...]
    g = jnp.dot(h, wg_ref[...], preferred_element_type=F32)
    u = jnp.dot(h, wu_ref[...], preferred_element_type=F32)
    o_ref[...] = (_silu(g) * u).astype(o_ref.dtype)


def _gate_up_call(h, wg, wu, w_down):
    s, d = h.shape
    f = wg.shape[1]
    bm, bn = _blk(2048, s), _blk(MXU_DIM, f)
    nj = f // bn
    wd_in, wd_out, wd_shape = _cast_rows_specs(w_down, (s // bm) * nj, lambda i, j: i * nj + j)
    return pl.pallas_call(
        _gate_up_kernel,
        grid=(s // bm, nj),
        in_specs=[pl.BlockSpec((bm, d), lambda i, j: (i, 0)),
                  pl.BlockSpec((d, bn), lambda i, j: (0, j)),
                  pl.BlockSpec((d, bn), lambda i, j: (0, j)),
                  wd_in],
        out_specs=[pl.BlockSpec((bm, bn), lambda i, j: (i, j)), wd_out],
        out_shape=[jax.ShapeDtypeStruct((s, f), BF16), wd_shape],
        compiler_params=_cparams("parallel", "arbitrary"),
        name="gate_up",
    )(h, wg, wu, w_down)


def _down_kernel(a_ref, w_ref, o_ref):
    o_ref[...] = jnp.dot(a_ref[...], w_ref[...], preferred_element_type=F32).astype(o_ref.dtype)


def _down_call(a, w):
    s, f = a.shape
    n = w.shape[1]
    bm, bn = _blk(512, s), _blk(512, n)
    return pl.pallas_call(
        _down_kernel,
        grid=(s // bm, n // bn),
        in_specs=[pl.BlockSpec((bm, f), lambda i, j: (i, 0)),
                  pl.BlockSpec((f, bn), lambda i, j: (0, j))],
        out_specs=pl.BlockSpec((bm, bn), lambda i, j: (i, j)),
        out_shape=jax.ShapeDtypeStruct((s, n), BF16),
        compiler_params=_cparams("parallel", "arbitrary"),
        name="down",
    )(a, w)


def _final_kernel(x_ref, y_ref, f_ref, gpost_ref, gffn_ref, gta_ref, gtf_ref, o_ref):
    x1 = _mixed_residual(x_ref, y_ref, gpost_ref, gta_ref)
    o_ref[...] = x1 + gtf_ref[...] * _rms(f_ref[...].astype(F32), gffn_ref[...])


def _final_call(x, y, f, g_post, g_ffn, mod_a, mod_f):
    s, d = x.shape
    bm = _blk(256, s)
    row = pl.BlockSpec((bm, d), lambda i: (i, 0))
    chunk = lambda c: pl.BlockSpec((1, d), lambda i: (0, c))
    return pl.pallas_call(
        _final_kernel,
        grid=(s // bm,),
        in_specs=[row, row, row, chunk(0), chunk(0), chunk(2), chunk(2)],
        out_specs=row,
        out_shape=jax.ShapeDtypeStruct((s, d), F32),
        compiler_params=_cparams("parallel"),
        name="final",
    )(x, y, f, g_post, g_ffn, mod_a, mod_f)


def _swap_halves(w, axis):
    lo, hi = jnp.split(w, 2, axis=axis)
    return jnp.concatenate([hi, lo], axis=axis)


def _layer(x, c_col, cs, cst, p):
    s, d = x.shape
    width = p["conv_w"].shape[-1]
    rq = p["g_q_lat"].shape[-1]
    rkv = p["g_kv_lat"].shape[-1]
    heads = p["w_kv_up"].shape[-1] // (QK_NOPE_DIM + V_HEAD_DIM)
    qk_dim = QK_NOPE_DIM + QK_ROPE_DIM

    b_mod = p["b_mod"][None, :]
    n_mod_a = (N_MOD // 2) * d
    mod = _mod_call(c_col, p["w_mod"], b_mod, n_mod_a)

    w_in_t = p["w_in"].T
    n_main = 2 * width + rq + rkv
    h, kr = _pre_mix_call(x, p["g_pre_mix"][None, :], mod, w_in_t, n_main)
    proj = _in_proj_call(h, w_in_t, n_main)

    wax = jnp.concatenate([p["lru_w_a"], p["lru_w_x"]], axis=-1).astype(BF16)
    y_lru = _lru_call(proj, p["conv_w"], p["conv_b"][None, :], wax, p["lru_b_a"][None, :],
                      p["lru_b_x"][None, :], p["lru_lambda"][None, :], width)

    wq_t = p["w_q_up"].T.reshape(heads, qk_dim, rq)
    wq_t = jnp.concatenate([wq_t, _swap_halves(wq_t[:, QK_NOPE_DIM:], 1)], axis=1).astype(BF16)
    wkv_t = p["w_kv_up"].T.reshape(heads, QK_NOPE_DIM + V_HEAD_DIM, rkv).astype(BF16)
    scale = float(qk_dim) ** -0.5 * math.log2(math.e)
    qt, k, vt = _qkv_call(proj, kr, cs, cst, p["g_q_lat"][None, :], p["g_kv_lat"][None, :], wq_t,
                          wkv_t[:, :QK_NOPE_DIM].reshape(heads // 2, 2 * QK_NOPE_DIM, rkv),
                          wkv_t[:, QK_NOPE_DIM:], 2 * width, 2 * width + rq, scale)
    y_mla, mod_f, (w_out, w_gate, w_up) = _attn_call(
        qt, k, vt, [p["w_out"], p["w_gate"], p["w_up"]], c_col, p["w_mod"], b_mod, n_mod_a)

    y = _out_proj_call(y_lru, y_mla, p["g_lru_out"][None, :], p["g_mla_out"][None, :], w_out)
    h2 = _post_mix_call(x, y, p["g_post_mix"][None, :], p["g_pre_ffn"][None, :], mod, mod_f)

    act, w_down = _gate_up_call(h2, w_gate, w_up, p["w_down"])
    f = _down_call(act, w_down)
    return _final_call(x, y, f, p["g_post_mix"][None, :], p["g_post_ffn"][None, :], mod, mod_f)


def kernel(x, c, positions, w_mod, b_mod, g_pre_mix, w_in, conv_w, conv_b, lru_w_a, lru_b_a, lru_w_x,
           lru_b_x, lru_lambda, g_q_lat, w_q_up, g_kv_lat, w_kv_up, g_lru_out, g_mla_out, w_out,
           g_post_mix, g_pre_ffn, w_gate, w_up, w_down, g_post_ffn):
    params = dict(w_mod=w_mod, b_mod=b_mod, g_pre_mix=g_pre_mix, w_in=w_in, conv_w=conv_w, conv_b=conv_b,
                  lru_w_a=lru_w_a, lru_b_a=lru_b_a, lru_w_x=lru_w_x, lru_b_x=lru_b_x, lru_lambda=lru_lambda,
                  g_q_lat=g_q_lat, w_q_up=w_q_up, g_kv_lat=g_kv_lat, w_kv_up=w_kv_up, g_lru_out=g_lru_out,
                  g_mla_out=g_mla_out, w_out=w_out, g_post_mix=g_post_mix, g_pre_ffn=g_pre_ffn,
                  w_gate=w_gate, w_up=w_up, w_down=w_down, g_post_ffn=g_post_ffn)
    batch, depth = x.shape[0], w_mod.shape[0]
    outs = []
    for b in range(batch):
        xb = x[b]
        c_col = c[b][:, None]
        cs, cst = _rope_call(positions[b])
        for l in range(depth):
            xb = _layer(xb, c_col, cs, cst, {name: w[l] for name, w in params.items()})
        outs.append(xb)
    return jnp.stack(outs, axis=0)
```

```python
import functools
import math

import jax
import jax.numpy as jnp
from jax import lax
from jax.experimental import pallas as pl
from jax.experimental.pallas import tpu as pltpu

F32 = jnp.float32
BF16 = jnp.bfloat16

EPS = 1e-6
CHUNK = 64
CONV_WIDTH = 4
LRU_C = 8.0
QK_NOPE_DIM = 128
QK_ROPE_DIM = 64
V_HEAD_DIM = 128
ROPE_THETA = 10000.0
N_MOD = 6

LANES = 128
SUBLANES = 8
BF16_ROWS = 16
MXU_DIM = 256
VMEM_LIMIT_BYTES = 58 * 1024 * 1024

QK_PAD_DIM = MXU_DIM
ROW_CHUNK = 32
ATTN_BLOCK = 512
ATTN_HEADS_PER_STEP = 2

NT_DIMS = (((1,), (1,)), ((), ()))


def _cparams(*sem):
    return pltpu.CompilerParams(dimension_semantics=sem, vmem_limit_bytes=VMEM_LIMIT_BYTES)


def _blk(pref, dim):
    b = min(pref, dim)
    while dim % b:
        b //= 2
    return b


def _rms(x, g):
    ms = jnp.mean(x * x, axis=-1, keepdims=True)
    return x * lax.rsqrt(ms + EPS) * g


def _for_row_chunks(n_rows, body):
    for r in range(0, n_rows, ROW_CHUNK):
        body(pl.ds(r, ROW_CHUNK))


def _sigmoid(x):
    return 0.5 * (1.0 + jnp.tanh(0.5 * x))


def _silu(x):
    return x * _sigmoid(x)


def _cast_rows_specs(w, n_steps, step_of):
    rows, cols = w.shape
    assert rows % n_steps == 0 and (rows // n_steps) % BF16_ROWS == 0, (w.shape, n_steps)
    slab = rows // n_steps
    spec = pl.BlockSpec((slab, cols), lambda *ids: (step_of(*ids), 0))
    return spec, spec, jax.ShapeDtypeStruct((rows, cols), BF16)


def _mod_kernel(c_ref, w_ref, b_ref, o_ref):
    k = pl.program_id(1)
    c = c_ref[...]
    s = c * jax.nn.sigmoid(c)
    part = jnp.sum(w_ref[...] * s, axis=0, keepdims=True)

    @pl.when(k == 0)
    def _():
        o_ref[...] = b_ref[...] + part

    @pl.when(k != 0)
    def _():
        o_ref[...] += part


def _mod_call(c_col, w_mod, b_mod, n):
    d = w_mod.shape[0]
    tk, bn = _blk(1024, d), _blk(2048, n)
    return pl.pallas_call(
        _mod_kernel,
        grid=(n // bn, d // tk),
        in_specs=[pl.BlockSpec((tk, 1), lambda j, k: (k, 0)),
                  pl.BlockSpec((tk, bn), lambda j, k: (k, j)),
                  pl.BlockSpec((1, bn), lambda j, k: (0, j))],
        out_specs=pl.BlockSpec((1, bn), lambda j, k: (0, j)),
        out_shape=jax.ShapeDtypeStruct((1, n), F32),
        compiler_params=_cparams("parallel", "arbitrary"),
        name="mod",
    )(c_col, w_mod, b_mod)


def _pre_mix_kernel(x_ref, g_ref, sh_ref, sc_ref, wkr_ref, h_ref, kr_ref):
    g, sc, sh = g_ref[...], 1.0 + sc_ref[...], sh_ref[...]

    def modulate(rows):
        h_ref[rows, :] = (_rms(x_ref[rows, :], g) * sc + sh).astype(BF16)

    _for_row_chunks(x_ref.shape[0], modulate)
    w = wkr_ref[...]
    half = w.shape[0] // 2
    pair = jnp.concatenate([w, w[half:], w[:half]], axis=0).astype(BF16)
    kr_ref[...] = lax.dot_general(h_ref[...], pair, NT_DIMS, preferred_element_type=F32)


def _pre_mix_call(x, g, mod, w_t, n_main):
    s, d = x.shape
    bm = _blk(512, s)
    assert n_main % QK_ROPE_DIM == 0 and w_t.shape[0] == n_main + QK_ROPE_DIM
    return pl.pallas_call(
        _pre_mix_kernel,
        grid=(s // bm,),
        in_specs=[pl.BlockSpec((bm, d), lambda i: (i, 0)),
                  pl.BlockSpec((1, d), lambda i: (0, 0)),
                  pl.BlockSpec((1, d), lambda i: (0, 0)),
                  pl.BlockSpec((1, d), lambda i: (0, 1)),
                  pl.BlockSpec((QK_ROPE_DIM, d), lambda i: (n_main // QK_ROPE_DIM, 0))],
        out_specs=[pl.BlockSpec((bm, d), lambda i: (i, 0)),
                   pl.BlockSpec((bm, LANES), lambda i: (i, 0))],
        out_shape=[jax.ShapeDtypeStruct((s, d), BF16),
                   jax.ShapeDtypeStruct((s, LANES), F32)],
        compiler_params=_cparams("parallel"),
        name="pre_mix",
    )(x, g, mod, mod, w_t)


def _in_proj_kernel(h_ref, w_ref, o_ref, wbf_ref):
    @pl.when(pl.program_id(1) == 0)
    def _():
        wbf_ref[...] = w_ref[...].astype(BF16)

    o_ref[...] = lax.dot_general(h_ref[...], wbf_ref[...], NT_DIMS,
                                 preferred_element_type=F32).astype(o_ref.dtype)


def _in_proj_call(h, w_t, n_main):
    s, d = h.shape
    bm, bn = _blk(1024, s), _blk(512, n_main)
    return pl.pallas_call(
        _in_proj_kernel,
        grid=(n_main // bn, s // bm),
        in_specs=[pl.BlockSpec((bm, d), lambda j, i: (i, 0)),
                  pl.BlockSpec((bn, d), lambda j, i: (j, 0))],
        out_specs=pl.BlockSpec((bm, bn), lambda j, i: (i, j)),
        out_shape=jax.ShapeDtypeStruct((s, n_main), BF16),
        scratch_shapes=[pltpu.VMEM((bn, d), BF16)],
        compiler_params=_cparams("parallel", "arbitrary"),
        name="in_proj",
    )(h, w_t)


def _rope_kernel(pos_ref, freq_ref, cs_ref, cst_ref):
    ang = freq_ref[...] * pos_ref[...].astype(F32)
    cos, sin = jnp.cos(ang), jnp.sin(ang)
    cst = jnp.concatenate([cos, cos, -sin, sin], axis=0)
    cst_ref[...] = cst
    cs_ref[...] = jnp.transpose(cst)


def _rope_call(pos):
    s = pos.shape[0]
    half = QK_ROPE_DIM // 2
    assert 4 * half == LANES
    inv_freq = ROPE_THETA ** (-jnp.arange(0, QK_ROPE_DIM, 2, dtype=F32) / QK_ROPE_DIM)
    ts = _blk(1024, s)
    return pl.pallas_call(
        _rope_kernel,
        grid=(s // ts,),
        in_specs=[pl.BlockSpec((1, ts), lambda i: (0, i)), pl.BlockSpec((half, 1), lambda i: (0, 0))],
        out_specs=[pl.BlockSpec((ts, LANES), lambda i: (i, 0)), pl.BlockSpec((LANES, ts), lambda i: (0, i))],
        out_shape=[jax.ShapeDtypeStruct((s, LANES), F32), jax.ShapeDtypeStruct((LANES, s), F32)],
        compiler_params=_cparams("parallel"),
        name="rope_tab",
    )(pos[None, :], inv_freq[:, None])


def _lru_kernel(xr_ref, gr_ref, cw_ref, cb_ref, wax_ref, ba_ref, bx_ref, lam_ref, o_ref,
                xbuf, a_scr, b_scr, alast_scr, blast_scr, carry_scr):
    t = pl.program_id(1)
    tlen = xr_ref.shape[0]
    ngrp = tlen // SUBLANES

    @pl.when(t == 0)
    def _():
        xbuf[0:SUBLANES, :] = jnp.zeros((SUBLANES, LANES), F32)
        carry_scr[...] = jnp.zeros((SUBLANES, LANES), F32)

    x = xr_ref[...].astype(F32)
    xbuf[SUBLANES:SUBLANES + tlen, :] = x
    cw = cw_ref[...]
    xc = cb_ref[...]
    for k in range(CONV_WIDTH - 1):
        off = SUBLANES - (CONV_WIDTH - 1) + k
        xc = xc + xbuf[pl.ds(off, tlen), :] * cw[k:k + 1, :]
    xc = xc + x * cw[CONV_WIDTH - 1:CONV_WIDTH, :]
    xbuf[0:SUBLANES, :] = xbuf[tlen:tlen + SUBLANES, :]

    ri = jnp.dot(xc.astype(BF16), wax_ref[0], preferred_element_type=F32)
    r = _sigmoid(ri[:, :LANES] + ba_ref[...])
    ig = _sigmoid(ri[:, LANES:] + bx_ref[...])
    lam = lam_ref[...]
    softplus_neg_lam = jnp.maximum(-lam, 0.0) + jnp.log1p(jnp.exp(-jnp.abs(lam)))
    log_a = (-LRU_C * r) * softplus_neg_lam
    a = jnp.exp(log_a)
    u = jnp.sqrt(-jnp.tanh(log_a) * (1.0 + a * a)) * (ig * xc)

    av = a.reshape(ngrp, SUBLANES, LANES)
    bv = u.reshape(ngrp, SUBLANES, LANES)
    row = lax.broadcasted_iota(jnp.int32, av.shape, 1)
    for sh in (1, 2, 4):
        valid = row >= sh
        bv = jnp.where(valid, av * pltpu.roll(bv, sh, axis=1) + bv, bv)
        av = jnp.where(valid, av * pltpu.roll(av, sh, axis=1), av)
    a_scr[...] = av
    b_scr[...] = bv
    alast_scr[...] = jnp.broadcast_to(av[:, SUBLANES - 1:SUBLANES, :], av.shape)
    blast_scr[...] = jnp.broadcast_to(bv[:, SUBLANES - 1:SUBLANES, :], bv.shape)

    def group(g, carry):
        b_scr[g] = a_scr[g] * carry + b_scr[g]
        return alast_scr[g] * carry + blast_scr[g]

    carry_scr[...] = lax.fori_loop(0, ngrp, group, carry_scr[...], unroll=8)
    h = b_scr[...].reshape(tlen, LANES)

    gt = gr_ref[...].astype(F32)
    gelu = 0.5 * gt * (1.0 + jnp.tanh(0.7978845608028654 * (gt + 0.044715 * (gt * gt * gt))))
    o_ref[...] = (h * gelu).astype(o_ref.dtype)


def _lru_call(proj, conv_w, conv_b, wax, b_a, b_x, lam, width):
    s = proj.shape[0]
    nblk = width // LANES
    ts = _blk(4096, s)
    grp_scratch = pltpu.VMEM((ts // SUBLANES, SUBLANES, LANES), F32)
    vec = lambda: pl.BlockSpec((1, LANES), lambda c, t: (0, c))
    return pl.pallas_call(
        _lru_kernel,
        grid=(nblk, s // ts),
        in_specs=[pl.BlockSpec((ts, LANES), lambda c, t: (t, c)),
                  pl.BlockSpec((ts, LANES), lambda c, t: (t, nblk + c)),
                  pl.BlockSpec((CONV_WIDTH, LANES), lambda c, t: (0, c)),
                  vec(),
                  pl.BlockSpec((1, LANES, 2 * LANES), lambda c, t: (c, 0, 0)),
                  vec(), vec(), vec()],
        out_specs=pl.BlockSpec((ts, LANES), lambda c, t: (t, c)),
        out_shape=jax.ShapeDtypeStruct((s, width), BF16),
        scratch_shapes=[pltpu.VMEM((ts + SUBLANES, LANES), F32),
                        grp_scratch, grp_scratch, grp_scratch, grp_scratch,
                        pltpu.VMEM((SUBLANES, LANES), F32)],
        compiler_params=_cparams("parallel", "arbitrary"),
        name="lru",
    )(proj, proj, conv_w, conv_b, wax, b_a, b_x, lam)


def _qkv_kernel(ql_ref, kvl_ref, kr_ref, cs_ref, cst_ref, gq_ref, gkv_ref, wqt_ref, wkt_ref, wvt_ref,
                qt_ref, k_ref, vt_ref, qn_scr, kvn_scr, krot_scr, *, scale):
    bm = ql_ref.shape[0]
    gq, gkv = gq_ref[...], gkv_ref[...]

    def normalise(rows):
        qn_scr[rows, :] = _rms(ql_ref[rows, :].astype(F32), gq).astype(BF16)
        kvn_scr[rows, :] = _rms(kvl_ref[rows, :].astype(F32), gkv).astype(BF16)
        prod = kr_ref[rows, :] * cs_ref[rows, :]
        rot = prod + pltpu.roll(prod, QK_ROPE_DIM, axis=1)
        lane = lax.broadcasted_iota(jnp.int32, rot.shape, 1)
        krot_scr[rows, :] = jnp.where(lane < QK_ROPE_DIM, rot, 0.0).astype(BF16)

    _for_row_chunks(bm, normalise)

    def head_pair(t):
        qn, kvn = qn_scr[...], kvn_scr[...]
        kn = lax.dot_general(kvn, wkt_ref[t], NT_DIMS, preferred_element_type=F32)
        for c in range(2):
            h = 2 * t + c
            qt = lax.dot_general(wqt_ref[h], qn, NT_DIMS, preferred_element_type=F32)
            qt_ref[h, :QK_NOPE_DIM, :] = (qt[:QK_NOPE_DIM] * scale).astype(BF16)
            pair = qt[QK_NOPE_DIM:] * cst_ref[...]
            rot_t = (pair[:QK_ROPE_DIM] + pair[QK_ROPE_DIM:]) * scale
            qt_ref[h, QK_NOPE_DIM:QK_NOPE_DIM + QK_ROPE_DIM, :] = rot_t.astype(BF16)
            qt_ref[h, QK_NOPE_DIM + QK_ROPE_DIM:, :] = jnp.zeros(
                (QK_PAD_DIM - QK_NOPE_DIM - QK_ROPE_DIM, bm), BF16)
            k_ref[h, :, :QK_NOPE_DIM] = kn[:, c * QK_NOPE_DIM:(c + 1) * QK_NOPE_DIM].astype(BF16)
            k_ref[h, :, QK_NOPE_DIM:] = krot_scr[...]
            vt = lax.dot_general(wvt_ref[h], kvn, NT_DIMS, preferred_element_type=F32)
            vt_ref[h, 0] = vt.astype(BF16)

    def head_quad(t, carry):
        head_pair(2 * t)
        head_pair(2 * t + 1)
        return carry

    lax.fori_loop(0, qt_ref.shape[0] // 4, head_quad, 0)


def _qkv_call(proj, kr, cs, cst, g_q, g_kv, wqt, wkt, wvt, q_off, kv_off, scale):
    s = proj.shape[0]
    heads, _, rq = wqt.shape
    rkv = wkt.shape[2]
    bm = _blk(ATTN_BLOCK, s)
    whole = lambda shape: pl.BlockSpec(shape, lambda i: (0,) * len(shape))
    return pl.pallas_call(
        functools.partial(_qkv_kernel, scale=scale),
        grid=(s // bm,),
        in_specs=[pl.BlockSpec((bm, rq), lambda i: (i, q_off // rq)),
                  pl.BlockSpec((bm, rkv), lambda i: (i, kv_off // rkv)),
                  pl.BlockSpec((bm, LANES), lambda i: (i, 0)),
                  pl.BlockSpec((bm, LANES), lambda i: (i, 0)),
                  pl.BlockSpec((LANES, bm), lambda i: (0, i)),
                  whole((1, rq)), whole((1, rkv)),
                  whole(wqt.shape), whole(wkt.shape), whole(wvt.shape)],
        out_specs=[pl.BlockSpec((heads, QK_PAD_DIM, bm), lambda i: (0, 0, i)),
                   pl.BlockSpec((heads, bm, QK_PAD_DIM), lambda i: (0, i, 0)),
                   pl.BlockSpec((heads, 1, V_HEAD_DIM, bm), lambda i: (0, i, 0, 0))],
        out_shape=[jax.ShapeDtypeStruct((heads, QK_PAD_DIM, s), BF16),
                   jax.ShapeDtypeStruct((heads, s, QK_PAD_DIM), BF16),
                   jax.ShapeDtypeStruct((heads, s // bm, V_HEAD_DIM, bm), BF16)],
        scratch_shapes=[pltpu.VMEM((bm, rq), BF16), pltpu.VMEM((bm, rkv), BF16),
                        pltpu.VMEM((bm, LANES), BF16)],
        compiler_params=_cparams("parallel"),
        name="qkv_proj",
    )(proj, proj, kr, cs, cst, g_q, g_kv, wqt, wkt, wvt)


def _attn_kernel(qt_ref, k_ref, vt_ref, c_ref, wm_ref, bm_ref, *rest, blk, n_cast):
    cast_in, o_ref, mod_ref = rest[:n_cast], rest[n_cast], rest[n_cast + 1]
    cast_out = rest[n_cast + 2:2 * n_cast + 2]
    sa_ref, sb_ref, smaxa_ref, smaxb_ref, m_ref, l_ref, acc_ref, modpart_ref = rest[2 * n_cast + 2:]
    i = pl.program_id(1)
    hpb = qt_ref.shape[0]

    def scores(j, buf):
        s_ref, smax_ref = buf
        rows = pl.ds(pl.multiple_of(j * blk, blk), blk)
        for c in range(hpb):
            s = jnp.dot(k_ref[c, rows, :], qt_ref[c], preferred_element_type=F32)
            s_ref[c] = s
            smax_ref[c] = jnp.max(s, axis=0, keepdims=True)

    def update(j, buf, masked):
        s_ref, smax_ref = buf
        for c in range(hpb):
            if masked:
                kc = lax.broadcasted_iota(jnp.int32, (blk, blk), 0) // CHUNK
                qc = lax.broadcasted_iota(jnp.int32, (blk, blk), 1) // CHUNK
                s = jnp.where(kc <= qc, s_ref[c], -jnp.inf)
                s_ref[c] = s
                smax = jnp.max(s, axis=0, keepdims=True)
            else:
                smax = smax_ref[c]
            m = m_ref[c]
            m_new = jnp.maximum(m, smax)
            alpha = jnp.exp2(m - m_new)
            p = jnp.exp2(s_ref[c] - m_new)
            m_ref[c] = m_new
            l_ref[c] = alpha * l_ref[c] + jnp.sum(p, axis=0, keepdims=True)
            acc_ref[c] = alpha * acc_ref[c] + jnp.dot(vt_ref[c, j], p.astype(BF16),
                                                      preferred_element_type=F32)

    def step(j, cur, nxt):
        scores(j + 1, nxt)
        update(j, cur, False)

    buf_a, buf_b = (sa_ref, smaxa_ref), (sb_ref, smaxb_ref)

    def pair(t, carry):
        step(2 * t, buf_a, buf_b)
        step(2 * t + 1, buf_b, buf_a)
        return carry

    m_ref[...] = jnp.full(m_ref.shape, -jnp.inf, F32)
    l_ref[...] = jnp.zeros(l_ref.shape, F32)
    acc_ref[...] = jnp.zeros(acc_ref.shape, F32)
    scores(0, buf_a)

    for src, dst in zip(cast_in, cast_out):
        dst[...] = src[...].astype(BF16)
    cvec = c_ref[...]
    silu_c = cvec * jax.nn.sigmoid(cvec)
    n_cols = wm_ref.shape[1]
    col_chunk = _blk(8 * LANES, n_cols)
    for col in range(0, n_cols, col_chunk):
        cols = slice(col, col + col_chunk)
        modpart_ref[:, cols] = jnp.sum(wm_ref[:, cols] * silu_c, axis=0, keepdims=True)

    def quad(t, carry):
        return pair(2 * t + 1, pair(2 * t, carry))

    lax.fori_loop(0, i // 4, quad, 0)

    @pl.when(i % 4 >= 2)
    def _():
        pair(i // 4 * 2, 0)

    @pl.when(i % 2 == 1)
    def _():
        step(i - 1, buf_a, buf_b)
        update(i, buf_b, True)

    @pl.when(i % 2 == 0)
    def _():
        update(i, buf_a, True)

    for c in range(hpb):
        inv_l = 1.0 / l_ref[c]
        o_ref[:, c * V_HEAD_DIM:(c + 1) * V_HEAD_DIM] = jnp.transpose(acc_ref[c] * inv_l).astype(o_ref.dtype)

    step_id = pl.program_id(0) * pl.num_programs(1) + i

    @pl.when(step_id == 0)
    def _():
        mod_ref[...] = bm_ref[...] + modpart_ref[...]

    @pl.when(step_id != 0)
    def _():
        mod_ref[...] += modpart_ref[...]


def _attn_call(qt, k, vt, cast_weights, c_col, w_mod, b_mod, n_mod_done):
    heads, _, s = qt.shape
    blk = vt.shape[3]
    assert blk % CHUNK == 0
    nq = s // blk
    hpb = ATTN_HEADS_PER_STEP
    assert heads % hpb == 0
    ng = heads // hpb
    step_of = lambda h, i: h * nq + i
    cast = [_cast_rows_specs(w, ng * nq, step_of) for w in cast_weights]
    d, n_mod = w_mod.shape
    assert n_mod == 2 * n_mod_done and d % (ng * nq) == 0 and (d // (ng * nq)) % SUBLANES == 0
    slab = d // (ng * nq)
    outs = pl.pallas_call(
        functools.partial(_attn_kernel, blk=blk, n_cast=len(cast)),
        grid=(ng, nq),
        in_specs=[pl.BlockSpec((hpb, QK_PAD_DIM, blk), lambda h, i: (h, 0, i)),
                  pl.BlockSpec((hpb, s, QK_PAD_DIM), lambda h, i: (h, 0, 0)),
                  pl.BlockSpec((hpb, nq, V_HEAD_DIM, blk), lambda h, i: (h, 0, 0, 0)),
                  pl.BlockSpec((slab, 1), lambda h, i: (step_of(h, i), 0)),
                  pl.BlockSpec((slab, n_mod_done), lambda h, i: (step_of(h, i), 1)),
                  pl.BlockSpec((1, n_mod_done), lambda h, i: (0, 1))]
                 + [c[0] for c in cast],
        out_specs=[pl.BlockSpec((blk, hpb * V_HEAD_DIM), lambda h, i: (i, h)),
                   pl.BlockSpec((1, n_mod_done), lambda h, i: (0, 0))] + [c[1] for c in cast],
        out_shape=[jax.ShapeDtypeStruct((s, heads * V_HEAD_DIM), BF16),
                   jax.ShapeDtypeStruct((1, n_mod_done), F32)] + [c[2] for c in cast],
        scratch_shapes=[pltpu.VMEM((hpb, blk, blk), F32), pltpu.VMEM((hpb, blk, blk), F32),
                        pltpu.VMEM((hpb, 1, blk), F32), pltpu.VMEM((hpb, 1, blk), F32),
                        pltpu.VMEM((hpb, 1, blk), F32), pltpu.VMEM((hpb, 1, blk), F32),
                        pltpu.VMEM((hpb, V_HEAD_DIM, blk), F32),
                        pltpu.VMEM((1, n_mod_done), F32)],
        compiler_params=_cparams("arbitrary", "arbitrary"),
        name="attn",
    )(qt, k, vt, c_col, w_mod, b_mod, *cast_weights)
    return outs[0], outs[1], outs[2:]


def _out_proj_kernel(yl_ref, ym_ref, gl_ref, gm_ref, w_ref, o_ref, lhs_ref):
    width = yl_ref.shape[1]

    @pl.when(pl.program_id(1) == 0)
    def _():
        gl, gm = gl_ref[...], gm_ref[...]

        def normalise(rows):
            lhs_ref[rows, :width] = _rms(yl_ref[rows, :].astype(F32), gl).astype(BF16)
            lhs_ref[rows, width:] = _rms(ym_ref[rows, :].astype(F32), gm).astype(BF16)

        _for_row_chunks(yl_ref.shape[0], normalise)

    o_ref[...] = jnp.dot(lhs_ref[...], w_ref[...], preferred_element_type=F32).astype(o_ref.dtype)


def _out_proj_call(y_lru, y_mla, g_lru, g_mla, w):
    s, width = y_lru.shape
    wm = y_mla.shape[1]
    k, n = w.shape
    bm, bn = _blk(512, s), _blk(2048, n)
    return pl.pallas_call(
        _out_proj_kernel,
        grid=(s // bm, n // bn),
        in_specs=[pl.BlockSpec((bm, width), lambda i, j: (i, 0)),
                  pl.BlockSpec((bm, wm), lambda i, j: (i, 0)),
                  pl.BlockSpec((1, width), lambda i, j: (0, 0)),
                  pl.BlockSpec((1, wm), lambda i, j: (0, 0)),
                  pl.BlockSpec((k, bn), lambda i, j: (0, j))],
        out_specs=pl.BlockSpec((bm, bn), lambda i, j: (i, j)),
        out_shape=jax.ShapeDtypeStruct((s, n), BF16),
        scratch_shapes=[pltpu.VMEM((bm, k), BF16)],
        compiler_params=_cparams("parallel", "arbitrary"),
        name="out_proj",
    )(y_lru, y_mla, g_lru, g_mla, w)


def _mixed_residual(x_ref, y_ref, rows, gain):
    return x_ref[rows, :] + _rms(y_ref[rows, :].astype(F32), gain)


def _post_mix_kernel(x_ref, y_ref, gpost_ref, gpre_ref, gt_ref, sh_ref, sc_ref, h_ref):
    gain, gpre, sc, sh = gt_ref[...] * gpost_ref[...], gpre_ref[...], 1.0 + sc_ref[...], sh_ref[...]

    def modulate(rows):
        x1 = _mixed_residual(x_ref, y_ref, rows, gain)
        h_ref[rows, :] = (_rms(x1, gpre) * sc + sh).astype(BF16)

    _for_row_chunks(x_ref.shape[0], modulate)


def _post_mix_call(x, y, g_post, g_pre, mod_a, mod_f):
    s, d = x.shape
    bm = _blk(512, s)
    row = pl.BlockSpec((bm, d), lambda i: (i, 0))
    chunk = lambda c: pl.BlockSpec((1, d), lambda i: (0, c))
    return pl.pallas_call(
        _post_mix_kernel,
        grid=(s // bm,),
        in_specs=[row, row, chunk(0), chunk(0), chunk(2), chunk(0), chunk(1)],
        out_specs=row,
        out_shape=jax.ShapeDtypeStruct((s, d), BF16),
        compiler_params=_cparams("parallel"),
        name="post_mix",
    )(x, y, g_post, g_pre, mod_a, mod_f, mod_f)


def _gate_up_kernel(h_ref, wg_ref, wu_ref, wd_ref, o_ref, wd_out_ref):
    wd_out_ref[...] = wd_ref[...].astype(BF16)
    h = h_ref[...]
    g = jnp.dot(h, wg_ref[...], preferred_element_type=F32)
    u = jnp.dot(h, wu_ref[...], preferred_element_type=F32)
    o_ref[...] = (_silu(g) * u).astype(o_ref.dtype)


def _gate_up_call(h, wg, wu, w_down):
    s, d = h.shape
    f = wg.shape[1]
    bm, bn = _blk(2048, s), _blk(MXU_DIM, f)
    nj = f // bn
    wd_in, wd_out, wd_shape = _cast_rows_specs(w_down, (s // bm) * nj, lambda i, j: i * nj + j)
    return pl.pallas_call(
        _gate_up_kernel,
        grid=(s // bm, nj),
        in_specs=[pl.BlockSpec((bm, d), lambda i, j: (i, 0)),
                  pl.BlockSpec((d, bn), lambda i, j: (0, j)),
                  pl.BlockSpec((d, bn), lambda i, j: (0, j)),
                  wd_in],
        out_specs=[pl.BlockSpec((bm, bn), lambda i, j: (i, j)), wd_out],
        out_shape=[jax.ShapeDtypeStruct((s, f), BF16), wd_shape],
        compiler_params=_cparams("parallel", "arbitrary"),
        name="gate_up",
    )(h, wg, wu, w_down)


def _down_kernel(a_ref, w_ref, o_ref):
    o_ref[...] = jnp.dot(a_ref[...], w_ref[...], preferred_element_type=F32).astype(o_ref.dtype)


def _down_call(a, w):
    s, f = a.shape
    n = w.shape[1]
    bm, bn = _blk(512, s), _blk(512, n)
    return pl.pallas_call(
        _down_kernel,
        grid=(s // bm, n // bn),
        in_specs=[pl.BlockSpec((bm, f), lambda i, j: (i, 0)),
                  pl.BlockSpec((f, bn), lambda i, j: (0, j))],
        out_specs=pl.BlockSpec((bm, bn), lambda i, j: (i, j)),
        out_shape=jax.ShapeDtypeStruct((s, n), BF16),
        compiler_params=_cparams("parallel", "arbitrary"),
        name="down",
    )(a, w)


def _final_kernel(x_ref, y_ref, f_ref, gpost_ref, gffn_ref, gta_ref, gtf_ref, o_ref):
    gain_a, gain_f = gta_ref[...] * gpost_ref[...], gtf_ref[...] * gffn_ref[...]

    def residuals(rows):
        x1 = _mixed_residual(x_ref, y_ref, rows, gain_a)
        o_ref[rows, :] = x1 + _rms(f_ref[rows, :].astype(F32), gain_f)

    _for_row_chunks(x_ref.shape[0], residuals)


def _final_call(x, y, f, g_post, g_ffn, mod_a, mod_f):
    s, d = x.shape
    bm = _blk(256, s)
    row = pl.BlockSpec((bm, d), lambda i: (i, 0))
    chunk = lambda c: pl.BlockSpec((1, d), lambda i: (0, c))
    return pl.pallas_call(
        _final_kernel,
        grid=(s // bm,),
        in_specs=[row, row, row, chunk(0), chunk(0), chunk(2), chunk(2)],
        out_specs=row,
        out_shape=jax.ShapeDtypeStruct((s, d), F32),
        compiler_params=_cparams("parallel"),
        name="final",
    )(x, y, f, g_post, g_ffn, mod_a, mod_f)


def _swap_halves(w, axis):
    lo, hi = jnp.split(w, 2, axis=axis)
    return jnp.concatenate([hi, lo], axis=axis)


def _layer(x, c_col, cs, cst, p):
    s, d = x.shape
    width = p["conv_w"].shape[-1]
    rq = p["g_q_lat"].shape[-1]
    rkv = p["g_kv_lat"].shape[-1]
    heads = p["w_kv_up"].shape[-1] // (QK_NOPE_DIM + V_HEAD_DIM)
    qk_dim = QK_NOPE_DIM + QK_ROPE_DIM

    b_mod = p["b_mod"][None, :]
    n_mod_a = (N_MOD // 2) * d
    mod = _mod_call(c_col, p["w_mod"], b_mod, n_mod_a)

    w_in_t = p["w_in"].T
    n_main = 2 * width + rq + rkv
    h, kr = _pre_mix_call(x, p["g_pre_mix"][None, :], mod, w_in_t, n_main)
    proj = _in_proj_call(h, w_in_t, n_main)

    wax = jnp.concatenate([p["lru_w_a"], p["lru_w_x"]], axis=-1).astype(BF16)
    y_lru = _lru_call(proj, p["conv_w"], p["conv_b"][None, :], wax, p["lru_b_a"][None, :],
                      p["lru_b_x"][None, :], p["lru_lambda"][None, :], width)

    wq_t = p["w_q_up"].T.reshape(heads, qk_dim, rq)
    wq_t = jnp.concatenate([wq_t, _swap_halves(wq_t[:, QK_NOPE_DIM:], 1)], axis=1).astype(BF16)
    wkv_t = p["w_kv_up"].T.reshape(heads, QK_NOPE_DIM + V_HEAD_DIM, rkv).astype(BF16)
    scale = float(qk_dim) ** -0.5 * math.log2(math.e)
    qt, k, vt = _qkv_call(proj, kr, cs, cst, p["g_q_lat"][None, :], p["g_kv_lat"][None, :], wq_t,
                          wkv_t[:, :QK_NOPE_DIM].reshape(heads // 2, 2 * QK_NOPE_DIM, rkv),
                          wkv_t[:, QK_NOPE_DIM:], 2 * width, 2 * width + rq, scale)
    y_mla, mod_f, (w_out, w_gate, w_up) = _attn_call(
        qt, k, vt, [p["w_out"], p["w_gate"], p["w_up"]], c_col, p["w_mod"], b_mod, n_mod_a)

    y = _out_proj_call(y_lru, y_mla, p["g_lru_out"][None, :], p["g_mla_out"][None, :], w_out)
    h2 = _post_mix_call(x, y, p["g_post_mix"][None, :], p["g_pre_ffn"][None, :], mod, mod_f)

    act, w_down = _gate_up_call(h2, w_gate, w_up, p["w_down"])
    f = _down_call(act, w_down)
    return _final_call(x, y, f, p["g_post_mix"][None, :], p["g_post_ffn"][None, :], mod, mod_f)


def kernel(x, c, positions, w_mod, b_mod, g_pre_mix, w_in, conv_w, conv_b, lru_w_a, lru_b_a, lru_w_x,
           lru_b_x, lru_lambda, g_q_lat, w_q_up, g_kv_lat, w_kv_up, g_lru_out, g_mla_out, w_out,
           g_post_mix, g_pre_ffn, w_gate, w_up, w_down, g_post_ffn):
    params = dict(w_mod=w_mod, b_mod=b_mod, g_pre_mix=g_pre_mix, w_in=w_in, conv_w=conv_w, conv_b=conv_b,
                  lru_w_a=lru_w_a, lru_b_a=lru_b_a, lru_w_x=lru_w_x, lru_b_x=lru_b_x, lru_lambda=lru_lambda,
                  g_q_lat=g_q_lat, w_q_up=w_q_up, g_kv_lat=g_kv_lat, w_kv_up=w_kv_up, g_lru_out=g_lru_out,
                  g_mla_out=g_mla_out, w_out=w_out, g_post_mix=g_post_mix, g_pre_ffn=g_pre_ffn,
                  w_gate=w_gate, w_up=w_up, w_down=w_down, g_post_ffn=g_post_ffn)
    batch, depth = x.shape[0], w_mod.shape[0]
    outs = []
    for b in range(batch):
        xb = x[b]
        c_col = c[b][:, None]
        cs, cst = _rope_call(positions[b])
        for l in range(depth):
            xb = _layer(xb, c_col, cs, cst, {name: w[l] for name, w in params.items()})
        outs.append(xb)
    return jnp.stack(outs, axis=0)
```
